```python
import jax, jax.numpy as jnp
from jax import lax
import numpy as np

D_MODEL = 2048
BATCH = 1
SEQ = 8192
DEPTH = 4

N_A = DEPTH // 2
N_B = DEPTH - N_A
MIX = 3 * D_MODEL // 4
MEM_WIDTH = D_MODEL - MIX
POOL_WINDOWS = (2, 4, 8, 16)
N_POOL_GROUPS = len(POOL_WINDOWS)
POOL_CG = MIX // N_POOL_GROUPS
HEAD_DIM = 64
N_Q_HEADS = MIX // HEAD_DIM
N_KV_HEADS = 3
GQA_GROUP = N_Q_HEADS // N_KV_HEADS
WINDOW = 128
BLOCK = 128
MEM_LEN = 256
MEM_HEADS = 4
MEM_HEAD_DIM = MEM_WIDTH // MEM_HEADS
D_FF = 5632
ROPE_THETA = 10000.0
EPS = 1e-5

kernel_name = "yoco_pool_swa_sink_macaron_mem"


def rmsnorm(x, g):
    xf = x.astype(jnp.float32)
    y = xf * lax.rsqrt(jnp.mean(xf * xf, axis=-1, keepdims=True) + EPS)
    return (y * g.astype(jnp.float32)).astype(x.dtype)


def swiglu(x, w_gate, w_up, w_down):
    return (jax.nn.silu(x @ w_gate) * (x @ w_up)) @ w_down


def rope(x, positions):
    hd = x.shape[-1]
    half = hd // 2
    inv_freq = 1.0 / (ROPE_THETA ** (jnp.arange(half, dtype=jnp.float32) * (2.0 / hd)))
    ang = positions.astype(jnp.float32)[..., None] * inv_freq
    cos = jnp.cos(ang)[:, :, None, :]
    sin = jnp.sin(ang)[:, :, None, :]
    xf = x.astype(jnp.float32)
    x1, x2 = xf[..., :half], xf[..., half:]
    out = jnp.concatenate([x1 * cos - x2 * sin, x2 * cos + x1 * sin], axis=-1)
    return out.astype(x.dtype)


def pool_mixer(u, w_grp, scale):
    B, S, C = u.shape
    uf = u.astype(jnp.float32).reshape(B, S, N_POOL_GROUPS, POOL_CG)
    cs = jnp.cumsum(uf, axis=1)
    t1 = jnp.arange(S, dtype=jnp.float32) + 1.0
    pooled = []
    for g, w in enumerate(POOL_WINDOWS):
        c = cs[:, :, g]
        shifted = jnp.pad(c, ((0, 0), (w, 0), (0, 0)))[:, :S]
        cnt = jnp.minimum(t1, float(w))[None, :, None]
        pooled.append((c - shifted) / cnt)
    diff = (jnp.stack(pooled, axis=2) - uf).astype(u.dtype)
    y = jnp.einsum('bsgc,gcd->bsgd', diff, w_grp).reshape(B, S, C)
    return y * scale


def swa_sink_attention(q, k, v, sinks):
    B, S, _, hd = q.shape
    nb = S // BLOCK
    qb = q.reshape(B, nb, BLOCK, N_KV_HEADS, GQA_GROUP, hd)
    kb = k.reshape(B, nb, BLOCK, N_KV_HEADS, hd)
    vb = v.reshape(B, nb, BLOCK, N_KV_HEADS, hd)
    pad = ((0, 0), (1, 0), (0, 0), (0, 0), (0, 0))
    kk = jnp.concatenate([jnp.pad(kb, pad)[:, :nb], kb], axis=2)
    vv = jnp.concatenate([jnp.pad(vb, pad)[:, :nb], vb], axis=2)
    s = jnp.einsum('bnqkgd,bnjkd->bnkgqj', qb, kk).astype(jnp.float32) * (hd ** -0.5)
    qi = jnp.arange(BLOCK)[:, None] + BLOCK
    kj = jnp.arange(2 * BLOCK)[None, :]
    rel = qi - kj
    band = (rel >= 0) & (rel < WINDOW)
    first = (jnp.arange(nb)[:, None, None] > 0) | (kj[None] >= BLOCK)
    valid = band[None] & first
    s = jnp.where(valid[None, :, None, None], s, -jnp.inf)
    sk = sinks.astype(jnp.float32).reshape(N_KV_HEADS, GQA_GROUP)[None, None, :, :, None]
    m = jnp.maximum(jnp.max(s, axis=-1), sk)
    p = jnp.exp(s - m[..., None])
    denom = jnp.sum(p, axis=-1) + jnp.exp(sk - m)
    p = (p / denom[..., None]).astype(v.dtype)
    o = jnp.einsum('bnkgqj,bnjkd->bnqkgd', p, vv)
    return o.reshape(B, S, N_Q_HEADS * hd)


def mem_attention(qm, mk, mv):
    B, S, _ = qm.shape
    q = qm.reshape(B, S, MEM_HEADS, MEM_HEAD_DIM)
    k = mk.reshape(B, -1, MEM_HEADS, MEM_HEAD_DIM)
    v = mv.reshape(B, -1, MEM_HEADS, MEM_HEAD_DIM)
    s = jnp.einsum('bshd,bmhd->bhsm', q, k).astype(jnp.float32) * (MEM_HEAD_DIM ** -0.5)
    p = jax.nn.softmax(s, axis=-1).astype(v.dtype)
    return jnp.einsum('bhsm,bmhd->bshd', p, v).reshape(B, S, MEM_WIDTH)


def setup_inputs(seed: int = 0) -> dict:
    key = jax.random.key(seed)
    ks = jax.random.split(key, 20)
    f32 = jnp.float32
    nrm = lambda k, shape, fan_in: jax.random.normal(k, shape, f32) * (fan_in ** -0.5)
    x = jax.random.normal(ks[0], (BATCH, SEQ, D_MODEL), f32)
    mem = jax.random.normal(ks[1], (BATCH, MEM_LEN, D_MODEL), f32)
    offset = jax.random.randint(ks[2], (BATCH, 1), 0, 1024, dtype=jnp.int32)
    positions = offset + jnp.arange(SEQ, dtype=jnp.int32)[None, :]
    norms = 1.0 + 0.02 * jax.random.normal(ks[3], (DEPTH, 3, D_MODEL), f32)
    ffn_gate = nrm(ks[4], (DEPTH, 2, D_MODEL, D_FF), D_MODEL)
    ffn_up = nrm(ks[5], (DEPTH, 2, D_MODEL, D_FF), D_MODEL)
    ffn_down = nrm(ks[6], (DEPTH, 2, D_FF, D_MODEL), D_FF)
    w_in = nrm(ks[7], (DEPTH, D_MODEL, MIX + MEM_WIDTH), D_MODEL)
    w_out = nrm(ks[8], (DEPTH, MIX + MEM_WIDTH, D_MODEL), MIX + MEM_WIDTH)
    pool_w = nrm(ks[9], (N_A, N_POOL_GROUPS, POOL_CG, POOL_CG), POOL_CG)
    pool_scale = 1.0 + 0.1 * jax.random.normal(ks[10], (N_A, MIX), f32)
    w_kv = nrm(ks[11], (D_MODEL, 2 * N_KV_HEADS * HEAD_DIM), D_MODEL)
    kv_norm = 1.0 + 0.02 * jax.random.normal(ks[12], (D_MODEL,), f32)
    sinks = 0.5 * jax.random.normal(ks[13], (N_B, N_Q_HEADS), f32)
    w_mem_kv = nrm(ks[14], (DEPTH, D_MODEL, 2 * MEM_WIDTH), D_MODEL)
    mem_norm = 1.0 + 0.02 * jax.random.normal(ks[15], (D_MODEL,), f32)
    final_norm = 1.0 + 0.02 * jax.random.normal(ks[16], (D_MODEL,), f32)
    return {"x": x, "mem": mem, "positions": positions, "norms": norms,
            "ffn_gate": ffn_gate, "ffn_up": ffn_up, "ffn_down": ffn_down,
            "w_in": w_in, "w_out": w_out, "pool_w": pool_w, "pool_scale": pool_scale,
            "w_kv": w_kv, "kv_norm": kv_norm, "sinks": sinks, "w_mem_kv": w_mem_kv,
            "mem_norm": mem_norm, "final_norm": final_norm}


def reference(x, mem, positions, norms, ffn_gate, ffn_up, ffn_down, w_in, w_out,
              pool_w, pool_scale, w_kv, kv_norm, sinks, w_mem_kv, mem_norm, final_norm):
    B, S, _ = x.shape
    h = x
    mem_n = rmsnorm(mem, mem_norm)
    k_shared = None
    v_shared = None
    for l in range(DEPTH):
        h = h + 0.5 * swiglu(rmsnorm(h, norms[l, 0]), ffn_gate[l, 0], ffn_up[l, 0], ffn_down[l, 0])
        a = rmsnorm(h, norms[l, 1]) @ w_in[l]
        u, qm = a[..., :MIX], a[..., MIX:]
        if l < N_A:
            y = pool_mixer(u, pool_w[l], pool_scale[l])
        else:
            q = rope(u.reshape(B, S, N_Q_HEADS, HEAD_DIM), positions)
            y = swa_sink_attention(q, k_shared, v_shared, sinks[l - N_A])
        mkv = mem_n @ w_mem_kv[l]
        ym = mem_attention(qm, mkv[..., :MEM_WIDTH], mkv[..., MEM_WIDTH:])
        h = h + jnp.concatenate([y, ym], axis=-1) @ w_out[l]
        h = h + 0.5 * swiglu(rmsnorm(h, norms[l, 2]), ffn_gate[l, 1], ffn_up[l, 1], ffn_down[l, 1])
        if l == N_A - 1:
            kv = rmsnorm(h, kv_norm) @ w_kv
            kvw = N_KV_HEADS * HEAD_DIM
            k_shared = rope(kv[..., :kvw].reshape(B, S, N_KV_HEADS, HEAD_DIM), positions)
            v_shared = kv[..., kvw:].reshape(B, S, N_KV_HEADS, HEAD_DIM)
    return rmsnorm(h, final_norm)
```

```python
import functools

import jax
import jax.numpy as jnp
from jax import lax
from jax.experimental import pallas as pl
from jax.experimental.pallas import tpu as pltpu

F32 = jnp.float32
BF16 = jnp.bfloat16

POOL_WINDOWS = (2, 4, 8, 16)
HEAD_DIM = 64
GQA_GROUP = 8
ATTN_BLOCK = 128
MEM_HEADS = 4
ROPE_THETA = 10000.0
EPS = 1e-5

LANES = 128
MAX_POOL_WINDOW = max(POOL_WINDOWS)

ROW_TILE = 512
FF_TILE = 512
VMEM_LIMIT = 48 * 1024 * 1024


def _params(*sem):
    return pltpu.CompilerParams(dimension_semantics=sem, vmem_limit_bytes=VMEM_LIMIT)


def _rms(x, gain):
    y = x * lax.rsqrt(jnp.mean(x * x, axis=-1, keepdims=True) + EPS)
    return y * gain


def _ffn_body(h_ref, g_ref, wg_ref, wu_ref, wd_ref, fg_ref, o_ref, xn_ref, *, n_ff, final_norm):
    j = pl.program_id(1)

    @pl.when(j == 0)
    def _():
        xn_ref[...] = _rms(h_ref[...], g_ref[...]).astype(BF16)
        o_ref[...] = jnp.zeros_like(o_ref)

    xn = xn_ref[...]
    g = jnp.dot(xn, wg_ref[...], preferred_element_type=F32)
    u = jnp.dot(xn, wu_ref[...], preferred_element_type=F32)
    act = (g / (1.0 + jnp.exp(-g))) * u
    o_ref[...] += jnp.dot(act.astype(BF16), wd_ref[...], preferred_element_type=F32)

    @pl.when(j == n_ff - 1)
    def _():
        y = h_ref[...] + 0.5 * o_ref[...]
        if final_norm:
            y = _rms(y, fg_ref[...])
        o_ref[...] = y


def _ffn(h, gain, wg, wu, wd, layer, half, final_gain=None):
    s, d = h.shape
    ff = wg.shape[-1]
    n_ff = ff // FF_TILE
    final_norm = final_gain is not None
    fg = final_gain if final_norm else gain
    body = functools.partial(_ffn_body, n_ff=n_ff, final_norm=final_norm)
    return pl.pallas_call(
        body,
        out_shape=jax.ShapeDtypeStruct((s, d), F32),
        grid=(s // ROW_TILE, n_ff),
        in_specs=[
            pl.BlockSpec((ROW_TILE, d), lambda i, j: (i, 0)),
            pl.BlockSpec((1, d), lambda i, j: (0, 0)),
            pl.BlockSpec((None, None, d, FF_TILE), lambda i, j: (layer, half, 0, j)),
            pl.BlockSpec((None, None, d, FF_TILE), lambda i, j: (layer, half, 0, j)),
            pl.BlockSpec((None, None, FF_TILE, d), lambda i, j: (layer, half, j, 0)),
            pl.BlockSpec((1, d), lambda i, j: (0, 0)),
        ],
        out_specs=pl.BlockSpec((ROW_TILE, d), lambda i, j: (i, 0)),
        scratch_shapes=[pltpu.VMEM((ROW_TILE, d), BF16)],
        compiler_params=_params("parallel", "arbitrary"),
        name="ffn",
    )(h, gain, wg, wu, wd, fg)


def _norm_proj_body(h_ref, g_ref, w_ref, o_ref):
    xn = _rms(h_ref[...], g_ref[...]).astype(BF16)
    o_ref[...] = jnp.dot(xn, w_ref[...], preferred_element_type=F32)


def _norm_proj(h, gain, w, layer):
    s, d = h.shape
    n = w.shape[-1]
    return pl.pallas_call(
        _norm_proj_body,
        out_shape=jax.ShapeDtypeStruct((s, n), F32),
        grid=(s // ROW_TILE,),
        in_specs=[
            pl.BlockSpec((ROW_TILE, d), lambda i: (i, 0)),
            pl.BlockSpec((1, d), lambda i: (0, 0)),
            pl.BlockSpec((None, d, n), lambda i: (layer, 0, 0)),
        ],
        out_specs=pl.BlockSpec((ROW_TILE, n), lambda i: (i, 0)),
        compiler_params=_params("parallel"),
        name="norm_proj",
    )(h, gain, w)


def _out_proj_body(y_ref, w_ref, h_ref, o_ref):
    o_ref[...] = h_ref[...] + jnp.dot(y_ref[...], w_ref[...], preferred_element_type=F32)


def _out_proj(y, w, h, layer):
    s, d = h.shape
    k = y.shape[-1]
    return pl.pallas_call(
        _out_proj_body,
        out_shape=jax.ShapeDtypeStruct((s, d), F32),
        grid=(s // ROW_TILE,),
        in_specs=[
            pl.BlockSpec((ROW_TILE, k), lambda i: (i, 0)),
            pl.BlockSpec((None, k, d), lambda i: (layer, 0, 0)),
            pl.BlockSpec((ROW_TILE, d), lambda i: (i, 0)),
        ],
        out_specs=pl.BlockSpec((ROW_TILE, d), lambda i: (i, 0)),
        compiler_params=_params("parallel"),
        name="out_proj",
    )(y, w, h)


def _mem_kv_body(mem_ref, g_ref, w_ref, o_ref):
    mn = _rms(mem_ref[...], g_ref[...]).astype(BF16)
    o_ref[...] = jnp.dot(mn, w_ref[...], preferred_element_type=F32).astype(BF16)


def _mem_kv(mem, gain, w):
    depth, d, n = w.shape
    m = mem.shape[0]
    return pl.pallas_call(
        _mem_kv_body,
        out_shape=jax.ShapeDtypeStruct((depth, m, n), BF16),
        grid=(depth,),
        in_specs=[
            pl.BlockSpec((m, d), lambda l: (0, 0)),
            pl.BlockSpec((1, d), lambda l: (0, 0)),
            pl.BlockSpec((None, d, n), lambda l: (l, 0, 0)),
        ],
        out_specs=pl.BlockSpec((None, m, n), lambda l: (l, 0, 0)),
        compiler_params=_params("parallel"),
        name="mem_kv",
    )(mem, gain, w)


def _mem_attention(a_ref, mkv_ref, y_ref, mix, mem_width):
    hd = mem_width // MEM_HEADS
    scale = hd ** -0.5
    for head in range(MEM_HEADS):
        lo = mix + hd * head
        q = a_ref[:, lo:lo + hd].astype(BF16)
        k = mkv_ref[:, hd * head:hd * (head + 1)]
        v = mkv_ref[:, mem_width + hd * head:mem_width + hd * (head + 1)]
        s = lax.dot_general(q, k, (((1,), (1,)), ((), ())), preferred_element_type=F32) * scale
        p = jnp.exp(s - jnp.max(s, axis=-1, keepdims=True))
        p = p / jnp.sum(p, axis=-1, keepdims=True)
        o = jnp.dot(p.astype(BF16), v, preferred_element_type=F32)
        y_ref[:, lo:lo + hd] = o.astype(y_ref.dtype)


def _pool_mix_body(a_ref, pw_ref, ps_ref, mkv_ref, y_ref, carry_ref, *, mix, mem_width):
    i = pl.program_id(0)
    tm = a_ref.shape[0]
    cg = mix // len(POOL_WINDOWS)

    @pl.when(i == 0)
    def _():
        carry_ref[...] = jnp.zeros_like(carry_ref)

    t1 = (i * tm + lax.broadcasted_iota(jnp.int32, (tm, 1), 0) + 1).astype(F32)
    for grp, w in enumerate(POOL_WINDOWS):
        sl = slice(grp * cg, (grp + 1) * cg)
        ug = a_ref[:, sl]
        win = jnp.concatenate([carry_ref[:, sl], ug], axis=0)
        shift = 1
        while shift < w:
            win = win + pltpu.roll(win, shift, 0)
            shift *= 2
        pooled = win[MAX_POOL_WINDOW:, :] * (1.0 / jnp.minimum(t1, float(w)))
        diff = (pooled - ug).astype(BF16)
        yg = jnp.dot(diff, pw_ref[grp], preferred_element_type=F32) * ps_ref[:, sl]
        y_ref[:, sl] = yg.astype(y_ref.dtype)
        carry_ref[:, sl] = ug[tm - MAX_POOL_WINDOW:, :]

    _mem_attention(a_ref, mkv_ref, y_ref, mix, mem_width)


def _pool_mix(a, pool_w, pool_scale, mkv, layer, mix):
    s, d = a.shape
    mem_width = d - mix
    m = mkv.shape[1]
    ng, cg, _ = pool_w.shape[1:]
    body = functools.partial(_pool_mix_body, mix=mix, mem_width=mem_width)
    return pl.pallas_call(
        body,
        out_shape=jax.ShapeDtypeStruct((s, d), BF16),
        grid=(s // ROW_TILE,),
        in_specs=[
            pl.BlockSpec((ROW_TILE, d), lambda i: (i, 0)),
            pl.BlockSpec((None, ng, cg, cg), lambda i: (layer, 0, 0, 0)),
            pl.BlockSpec((None, 1, mix), lambda i: (layer, 0, 0)),
            pl.BlockSpec((None, m, 2 * mem_width), lambda i: (layer, 0, 0)),
        ],
        out_specs=pl.BlockSpec((ROW_TILE, d), lambda i: (i, 0)),
        scratch_shapes=[pltpu.VMEM((MAX_POOL_WINDOW, mix), F32)],
        compiler_params=_params("arbitrary"),
        name="pool_mix",
    )(a, pool_w, pool_scale[:, None, :], mkv)


def _trig_body(pos_ref, invf_ref, c_ref, sa_ref, sb_ref):
    ang = pos_ref[...] * invf_ref[...]
    c = jnp.cos(ang)
    s = jnp.sin(ang)
    lane = lax.broadcasted_iota(jnp.int32, ang.shape, 1)
    first_half = (lane & (HEAD_DIM // 2)) == 0
    c_ref[...] = c
    sa_ref[...] = jnp.where(first_half, -s, 0.0)
    sb_ref[...] = jnp.where(first_half, 0.0, s)


def _trig_tables(pos, invf):
    s = pos.shape[0]
    tab = jax.ShapeDtypeStruct((s, LANES), F32)
    row = pl.BlockSpec((ROW_TILE, LANES), lambda i: (i, 0))
    return pl.pallas_call(
        _trig_body,
        out_shape=(tab, tab, tab),
        grid=(s // ROW_TILE,),
        in_specs=[
            pl.BlockSpec((ROW_TILE, 1), lambda i: (i, 0)),
            pl.BlockSpec((1, LANES), lambda i: (0, 0)),
        ],
        out_specs=(row, row, row),
        compiler_params=_params("parallel"),
        name="trig_tables",
    )(pos, invf)


def _rope_slab(x, c, sa, sb):
    half = HEAD_DIM // 2
    return x * c + pltpu.roll(x, LANES - half, 1) * sa + pltpu.roll(x, half, 1) * sb


def _kv_layout_body(kv_ref, c_ref, sa_ref, sb_ref, kdup_ref, vlo_ref, vhi_ref):
    c, sa, sb = c_ref[...], sa_ref[...], sb_ref[...]
    lane = lax.broadcasted_iota(jnp.int32, c.shape, 1)
    left = lane < HEAD_DIM
    slab0 = kv_ref[:, 0:LANES]
    slab1 = kv_ref[:, LANES:2 * LANES]
    slab2 = kv_ref[:, 2 * LANES:3 * LANES]
    r0 = _rope_slab(slab0, c, sa, sb)
    r1 = _rope_slab(slab1, c, sa, sb)
    r0s = pltpu.roll(r0, HEAD_DIM, 1)
    r1s = pltpu.roll(r1, HEAD_DIM, 1)
    s1s = pltpu.roll(slab1, HEAD_DIM, 1)
    s2s = pltpu.roll(slab2, HEAD_DIM, 1)
    zero = jnp.zeros_like(c)
    kdup_ref[0] = jnp.where(left, r0, r0s).astype(BF16)
    kdup_ref[1] = jnp.where(left, r0s, r0).astype(BF16)
    kdup_ref[2] = jnp.where(left, r1, r1s).astype(BF16)
    vlo_ref[0] = jnp.where(left, s1s, zero).astype(BF16)
    vhi_ref[0] = jnp.where(left, zero, slab1).astype(BF16)
    vlo_ref[1] = jnp.where(left, slab2, zero).astype(BF16)
    vhi_ref[1] = jnp.where(left, zero, s2s).astype(BF16)
    vlo_ref[2] = jnp.where(left, s2s, zero).astype(BF16)
    vhi_ref[2] = jnp.where(left, zero, slab2).astype(BF16)


def _kv_layout(kv, c, sa, sb):
    s, n = kv.shape
    n_kv = n // (2 * HEAD_DIM)
    assert n == 3 * LANES and n_kv == 3
    out = jax.ShapeDtypeStruct((n_kv, s, LANES), BF16)
    row = pl.BlockSpec((ROW_TILE, LANES), lambda i: (i, 0))
    slab = pl.BlockSpec((n_kv, ROW_TILE, LANES), lambda i: (0, i, 0))
    return pl.pallas_call(
        _kv_layout_body,
        out_shape=(out, out, out),
        grid=(s // ROW_TILE,),
        in_specs=[pl.BlockSpec((ROW_TILE, n), lambda i: (i, 0)), row, row, row],
        out_specs=(slab, slab, slab),
        compiler_params=_params("parallel"),
        name="kv_layout",
    )(kv, c, sa, sb)


def _swa_mix_body(sink_ref, a_ref, c_ref, sa_ref, sb_ref,
                  kp_ref, kc_ref, vlp_ref, vlc_ref, vhp_ref, vhc_ref, mkv_ref,
                  y_ref, qlo_ref, qhi_ref, kf_ref, vlf_ref, vhf_ref, *, mix, mem_width):
    i = pl.program_id(0)
    tm = a_ref.shape[0]
    blk = ATTN_BLOCK
    nb = tm // blk
    n_kv = kc_ref.shape[0]
    pairs_per_kv = GQA_GROUP // 2
    n_pairs = mix // LANES

    kf_ref[:, 0:blk, :] = kp_ref[...]
    kf_ref[:, blk:, :] = kc_ref[...]
    vlf_ref[:, 0:blk, :] = vlp_ref[...]
    vlf_ref[:, blk:, :] = vlc_ref[...]
    vhf_ref[:, 0:blk, :] = vhp_ref[...]
    vhf_ref[:, blk:, :] = vhc_ref[...]

    c, sa, sb = c_ref[...], sa_ref[...], sb_ref[...]
    lane = lax.broadcasted_iota(jnp.int32, c.shape, 1)
    left = lane < HEAD_DIM
    scale = HEAD_DIM ** -0.5
    for p in range(n_pairs):
        sl = slice(p * LANES, (p + 1) * LANES)
        q = _rope_slab(a_ref[:, sl], c, sa, sb) * scale
        qlo_ref[:, sl] = jnp.where(left, q, 0.0).astype(BF16)
        qhi_ref[:, sl] = jnp.where(left, 0.0, q).astype(BF16)

    r = lax.broadcasted_iota(jnp.int32, (blk, 2 * blk), 0)
    kj = lax.broadcasted_iota(jnp.int32, (blk, 2 * blk), 1)
    band = (kj > r) & (kj <= r + blk)

    def block_step(n, carry):
        r0 = pl.multiple_of(n * blk, blk)
        rows = pl.ds(r0, blk)
        keys = pl.ds(r0, 2 * blk)
        first_key = jnp.where(i * nb + n == 0, blk, 0)
        valid = band & (kj >= first_key)
        for kvh in range(n_kv):
            p0 = kvh * pairs_per_kv
            q_all = jnp.concatenate(
                [qlo_ref[rows, (p0 + p) * LANES:(p0 + p + 1) * LANES] for p in range(pairs_per_kv)]
                + [qhi_ref[rows, (p0 + p) * LANES:(p0 + p + 1) * LANES] for p in range(pairs_per_kv)],
                axis=0)
            s_all = lax.dot_general(q_all, kf_ref[kvh, keys, :], (((1,), (1,)), ((), ())),
                                    preferred_element_type=F32)
            probs = []
            for idx in range(GQA_GROUP):
                half, p = divmod(idx, pairs_per_kv)
                sink = sink_ref[2 * (p0 + p) + half]
                s = jnp.where(valid, s_all[idx * blk:(idx + 1) * blk, :], -jnp.inf)
                m = jnp.maximum(jnp.max(s, axis=-1, keepdims=True), sink)
                e = jnp.exp(s - m)
                denom = jnp.sum(e, axis=-1, keepdims=True) + jnp.exp(sink - m)
                probs.append((e / denom).astype(BF16))
            p_lo = jnp.concatenate(probs[:pairs_per_kv], axis=0)
            p_hi = jnp.concatenate(probs[pairs_per_kv:], axis=0)
            o = (jnp.dot(p_lo, vlf_ref[kvh, keys, :], preferred_element_type=F32)
                 + jnp.dot(p_hi, vhf_ref[kvh, keys, :], preferred_element_type=F32))
            for p in range(pairs_per_kv):
                y_ref[rows, (p0 + p) * LANES:(p0 + p + 1) * LANES] = (
                    o[p * blk:(p + 1) * blk, :].astype(y_ref.dtype))
        return carry

    lax.fori_loop(0, nb, block_step, 0)
    _mem_attention(a_ref, mkv_ref, y_ref, mix, mem_width)


def _swa_mix(a, sinks, c, sa, sb, kdup, vlo, vhi, mkv, layer, mix):
    s, d = a.shape
    mem_width = d - mix
    m = mkv.shape[1]
    n_kv = kdup.shape[0]
    nb = ROW_TILE // ATTN_BLOCK
    body = functools.partial(_swa_mix_body, mix=mix, mem_width=mem_width)
    row = pl.BlockSpec((ROW_TILE, LANES), lambda i: (i, 0))
    prev = pl.BlockSpec((n_kv, ATTN_BLOCK, LANES), lambda i: (0, jnp.maximum(i * nb - 1, 0), 0))
    cur = pl.BlockSpec((n_kv, ROW_TILE, LANES), lambda i: (0, i, 0))
    full = pltpu.VMEM((n_kv, ROW_TILE + ATTN_BLOCK, LANES), BF16)
    return pl.pallas_call(
        body,
        out_shape=jax.ShapeDtypeStruct((s, d), BF16),
        grid=(s // ROW_TILE,),
        in_specs=[
            pl.BlockSpec(memory_space=pltpu.SMEM),
            pl.BlockSpec((ROW_TILE, d), lambda i: (i, 0)),
            row, row, row,
            prev, cur, prev, cur, prev, cur,
            pl.BlockSpec((None, m, 2 * mem_width), lambda i: (layer, 0, 0)),
        ],
        out_specs=pl.BlockSpec((ROW_TILE, d), lambda i: (i, 0)),
        scratch_shapes=[
            pltpu.VMEM((ROW_TILE, mix), BF16),
            pltpu.VMEM((ROW_TILE, mix), BF16),
            full, full, full,
        ],
        compiler_params=_params("parallel"),
        name="swa_mix",
    )(sinks, a, c, sa, sb, kdup, kdup, vlo, vlo, vhi, vhi, mkv)


def kernel(x, mem, positions, norms, ffn_gate, ffn_up, ffn_down, w_in, w_out, pool_w, pool_scale,
           w_kv, kv_norm, sinks, w_mem_kv, mem_norm, final_norm):
    batch, seq, d = x.shape
    assert batch == 1
    depth = norms.shape[0]
    n_pool = pool_w.shape[0]
    mix = pool_scale.shape[-1]

    wg = ffn_gate.astype(BF16)
    wu = ffn_up.astype(BF16)
    wd = ffn_down.astype(BF16)
    w_in_b = w_in.astype(BF16)
    w_out_b = w_out.astype(BF16)
    pool_w_b = pool_w.astype(BF16)
    w_kv_b = w_kv.astype(BF16)
    w_mem_kv_b = w_mem_kv.astype(BF16)

    half = HEAD_DIM // 2
    inv_freq = 1.0 / (ROPE_THETA ** (jnp.arange(half, dtype=F32) * (2.0 / HEAD_DIM)))
    invf = jnp.tile(inv_freq, LANES // half)[None, :]
    pos = positions[0].astype(F32)[:, None]
    cos_t, sin_a, sin_b = _trig_tables(pos, invf)

    mkv = _mem_kv(mem[0], mem_norm[None, :], w_mem_kv_b)

    h = x[0]
    kdup = vlo = vhi = None
    for l in range(depth):
        h = _ffn(h, norms[l, 0][None, :], wg, wu, wd, l, 0)
        a = _norm_proj(h, norms[l, 1][None, :], w_in_b, l)
        if l < n_pool:
            y = _pool_mix(a, pool_w_b, pool_scale, mkv, l, mix)
        else:
            y = _swa_mix(a, sinks[l - n_pool], cos_t, sin_a, sin_b, kdup, vlo, vhi, mkv, l, mix)
        h = _out_proj(y, w_out_b, h, l)
        last = l == depth - 1
        h = _ffn(h, norms[l, 2][None, :], wg, wu, wd, l, 1,
                 final_gain=final_norm[None, :] if last else None)
        if l == n_pool - 1:
            kv = _norm_proj(h, kv_norm[None, :], w_kv_b[None], 0)
            kdup, vlo, vhi = _kv_layout(kv, cos_t, sin_a, sin_b)
    return h[None]
```

```python
import functools

import jax
import jax.numpy as jnp
from jax import lax
from jax.experimental import pallas as pl
from jax.experimental.pallas import tpu as pltpu

F32 = jnp.float32
BF16 = jnp.bfloat16

POOL_WINDOWS = (2, 4, 8, 16)
HEAD_DIM = 64
GQA_GROUP = 8
ATTN_BLOCK = 128
MEM_HEADS = 4
ROPE_THETA = 10000.0
EPS = 1e-5

LANES = 128
MAX_POOL_WINDOW = max(POOL_WINDOWS)

ROW_TILE = 512
FFN_ROW_TILE = 1024
FF_TILE = 512
VMEM_LIMIT = 48 * 1024 * 1024
FFN_VMEM_LIMIT = 56 * 1024 * 1024


def _params(*sem):
    return pltpu.CompilerParams(dimension_semantics=sem, vmem_limit_bytes=VMEM_LIMIT)


def _rms(x, gain):
    y = x * lax.rsqrt(jnp.mean(x * x, axis=-1, keepdims=True) + EPS)
    return y * gain


def _ffn_body(h_hbm, g_ref, wg_ref, wu_ref, wd_ref, fg_ref, o_hbm, acc_ref, xn_ref, in_sem, out_sem,
              *, n_rows, n_ff, final_norm):
    i = pl.program_id(0)
    j = pl.program_id(1)
    tm = acc_ref.shape[1]
    slot = lax.rem(i, 2)
    other = 1 - slot

    def in_copy(tile, s):
        return pltpu.make_async_copy(h_hbm.at[pl.ds(tile * tm, tm), :], acc_ref.at[s], in_sem.at[s])

    def out_copy(tile, s):
        return pltpu.make_async_copy(acc_ref.at[s], o_hbm.at[pl.ds(tile * tm, tm), :], out_sem.at[s])

    @pl.when((i == 0) & (j == 0))
    def _():
        in_copy(0, 0).start()

    @pl.when(j == 0)
    def _():
        in_copy(i, slot).wait()
        xn_ref[...] = _rms(acc_ref[slot], g_ref[...]).astype(BF16)

    @pl.when((j == 1) & (i + 1 < n_rows))
    def _():
        @pl.when(i >= 1)
        def _():
            out_copy(i - 1, other).wait()

        in_copy(i + 1, other).start()

    xn = xn_ref[...]
    g = jnp.dot(xn, wg_ref[...].astype(BF16), preferred_element_type=F32)
    u = jnp.dot(xn, wu_ref[...].astype(BF16), preferred_element_type=F32)
    act = (g / (1.0 + jnp.exp(-g))) * (0.5 * u)
    acc_ref[slot] += jnp.dot(act.astype(BF16), wd_ref[...].astype(BF16), preferred_element_type=F32)

    @pl.when(j == n_ff - 1)
    def _():
        if final_norm:
            acc_ref[slot] = _rms(acc_ref[slot], fg_ref[...])
        out_copy(i, slot).start()

        @pl.when(i == n_rows - 1)
        def _():
            out_copy(i, slot).wait()
            if n_rows >= 2:
                out_copy(i - 1, other).wait()


def _ffn(h, gain, wg, wu, wd, layer, half, final_gain=None):
    s, d = h.shape
    ff = wg.shape[-1]
    n_ff = ff // FF_TILE
    n_rows = s // FFN_ROW_TILE
    assert n_ff >= 2 and s % FFN_ROW_TILE == 0 and ff % FF_TILE == 0
    final_norm = final_gain is not None
    fg = final_gain if final_norm else gain
    body = functools.partial(_ffn_body, n_rows=n_rows, n_ff=n_ff, final_norm=final_norm)
    return pl.pallas_call(
        body,
        out_shape=jax.ShapeDtypeStruct((s, d), F32),
        grid=(n_rows, n_ff),
        in_specs=[
            pl.BlockSpec(memory_space=pl.ANY),
            pl.BlockSpec((1, d), lambda i, j: (0, 0)),
            pl.BlockSpec((None, None, d, FF_TILE), lambda i, j: (layer, half, 0, j)),
            pl.BlockSpec((None, None, d, FF_TILE), lambda i, j: (layer, half, 0, j)),
            pl.BlockSpec((None, None, FF_TILE, d), lambda i, j: (layer, half, j, 0)),
            pl.BlockSpec((1, d), lambda i, j: (0, 0)),
        ],
        out_specs=pl.BlockSpec(memory_space=pl.ANY),
        scratch_shapes=[
            pltpu.VMEM((2, FFN_ROW_TILE, d), F32),
            pltpu.VMEM((FFN_ROW_TILE, d), BF16),
            pltpu.SemaphoreType.DMA((2,)),
            pltpu.SemaphoreType.DMA((2,)),
        ],
        compiler_params=pltpu.CompilerParams(
            dimension_semantics=("arbitrary", "arbitrary"), vmem_limit_bytes=FFN_VMEM_LIMIT),
        name="ffn",
    )(h, gain, wg, wu, wd, fg)


def _norm_proj_body(h_ref, g_ref, w_ref, o_ref):
    xn = _rms(h_ref[...], g_ref[...]).astype(BF16)
    o_ref[...] = jnp.dot(xn, w_ref[...].astype(BF16), preferred_element_type=F32)


def _norm_proj(h, gain, w, layer):
    s, d = h.shape
    n = w.shape[-1]
    return pl.pallas_call(
        _norm_proj_body,
        out_shape=jax.ShapeDtypeStruct((s, n), F32),
        grid=(s // ROW_TILE,),
        in_specs=[
            pl.BlockSpec((ROW_TILE, d), lambda i: (i, 0)),
            pl.BlockSpec((1, d), lambda i: (0, 0)),
            pl.BlockSpec((None, d, n), lambda i: (layer, 0, 0), pipeline_mode=pl.Buffered(1)),
        ],
        out_specs=pl.BlockSpec((ROW_TILE, n), lambda i: (i, 0)),
        compiler_params=_params("parallel"),
        name="norm_proj",
    )(h, gain, w)


def _out_proj_body(y_ref, w_ref, h_ref, o_ref):
    o_ref[...] = h_ref[...] + jnp.dot(y_ref[...], w_ref[...].astype(BF16), preferred_element_type=F32)


def _out_proj(y, w, h, layer):
    s, d = h.shape
    k = y.shape[-1]
    return pl.pallas_call(
        _out_proj_body,
        out_shape=jax.ShapeDtypeStruct((s, d), F32),
        grid=(s // ROW_TILE,),
        in_specs=[
            pl.BlockSpec((ROW_TILE, k), lambda i: (i, 0)),
            pl.BlockSpec((None, k, d), lambda i: (layer, 0, 0), pipeline_mode=pl.Buffered(1)),
            pl.BlockSpec((ROW_TILE, d), lambda i: (i, 0)),
        ],
        out_specs=pl.BlockSpec((ROW_TILE, d), lambda i: (i, 0)),
        compiler_params=_params("parallel"),
        name="out_proj",
    )(y, w, h)


def _mem_kv_body(mem_ref, g_ref, w_ref, o_ref):
    mn = _rms(mem_ref[...], g_ref[...]).astype(BF16)
    o_ref[...] = jnp.dot(mn, w_ref[...].astype(BF16), preferred_element_type=F32).astype(BF16)


def _mem_kv(mem, gain, w):
    depth, d, n = w.shape
    m = mem.shape[0]
    return pl.pallas_call(
        _mem_kv_body,
        out_shape=jax.ShapeDtypeStruct((depth, m, n), BF16),
        grid=(depth,),
        in_specs=[
            pl.BlockSpec((m, d), lambda l: (0, 0)),
            pl.BlockSpec((1, d), lambda l: (0, 0)),
            pl.BlockSpec((None, d, n), lambda l: (l, 0, 0)),
        ],
        out_specs=pl.BlockSpec((None, m, n), lambda l: (l, 0, 0)),
        compiler_params=_params("parallel"),
        name="mem_kv",
    )(mem, gain, w)


def _mem_attention(a_ref, mkv_ref, y_ref, mix, mem_width):
    hd = mem_width // MEM_HEADS
    scale = hd ** -0.5
    for head in range(MEM_HEADS):
        lo = mix + hd * head
        q = a_ref[:, lo:lo + hd].astype(BF16)
        k = mkv_ref[:, hd * head:hd * (head + 1)]
        v = mkv_ref[:, mem_width + hd * head:mem_width + hd * (head + 1)]
        s = lax.dot_general(q, k, (((1,), (1,)), ((), ())), preferred_element_type=F32) * scale
        p = jnp.exp(s - jnp.max(s, axis=-1, keepdims=True))
        p = p / jnp.sum(p, axis=-1, keepdims=True)
        o = jnp.dot(p.astype(BF16), v, preferred_element_type=F32)
        y_ref[:, lo:lo + hd] = o.astype(y_ref.dtype)


def _pool_mix_body(a_ref, pw_ref, ps_ref, mkv_ref, y_ref, carry_ref, *, mix, mem_width):
    i = pl.program_id(0)
    tm = a_ref.shape[0]
    cg = mix // len(POOL_WINDOWS)

    @pl.when(i == 0)
    def _():
        carry_ref[...] = jnp.zeros_like(carry_ref)

    t1 = (i * tm + lax.broadcasted_iota(jnp.int32, (tm, 1), 0) + 1).astype(F32)
    for grp, w in enumerate(POOL_WINDOWS):
        sl = slice(grp * cg, (grp + 1) * cg)
        ug = a_ref[:, sl]
        win = jnp.concatenate([carry_ref[:, sl], ug], axis=0)
        shift = 1
        while shift < w:
            win = win + pltpu.roll(win, shift, 0)
            shift *= 2
        pooled = win[MAX_POOL_WINDOW:, :] * (1.0 / jnp.minimum(t1, float(w)))
        diff = (pooled - ug).astype(BF16)
        yg = jnp.dot(diff, pw_ref[grp].astype(BF16), preferred_element_type=F32) * ps_ref[:, sl]
        y_ref[:, sl] = yg.astype(y_ref.dtype)
        carry_ref[:, sl] = ug[tm - MAX_POOL_WINDOW:, :]

    _mem_attention(a_ref, mkv_ref, y_ref, mix, mem_width)


def _pool_mix(a, pool_w, pool_scale, mkv, layer, mix):
    s, d = a.shape
    mem_width = d - mix
    m = mkv.shape[1]
    ng, cg, _ = pool_w.shape[1:]
    body = functools.partial(_pool_mix_body, mix=mix, mem_width=mem_width)
    return pl.pallas_call(
        body,
        out_shape=jax.ShapeDtypeStruct((s, d), BF16),
        grid=(s // ROW_TILE,),
        in_specs=[
            pl.BlockSpec((ROW_TILE, d), lambda i: (i, 0)),
            pl.BlockSpec((None, ng, cg, cg), lambda i: (layer, 0, 0, 0)),
            pl.BlockSpec((None, 1, mix), lambda i: (layer, 0, 0)),
            pl.BlockSpec((None, m, 2 * mem_width), lambda i: (layer, 0, 0)),
        ],
        out_specs=pl.BlockSpec((ROW_TILE, d), lambda i: (i, 0)),
        scratch_shapes=[pltpu.VMEM((MAX_POOL_WINDOW, mix), F32)],
        compiler_params=_params("arbitrary"),
        name="pool_mix",
    )(a, pool_w, pool_scale[:, None, :], mkv)


def _trig_body(pos_ref, invf_ref, c_ref, sa_ref, sb_ref):
    ang = pos_ref[...] * invf_ref[...]
    c = jnp.cos(ang)
    s = jnp.sin(ang)
    lane = lax.broadcasted_iota(jnp.int32, ang.shape, 1)
    first_half = (lane & (HEAD_DIM // 2)) == 0
    c_ref[...] = c
    sa_ref[...] = jnp.where(first_half, -s, 0.0)
    sb_ref[...] = jnp.where(first_half, 0.0, s)


def _trig_tables(pos, invf):
    s = pos.shape[0]
    tab = jax.ShapeDtypeStruct((s, LANES), F32)
    row = pl.BlockSpec((ROW_TILE, LANES), lambda i: (i, 0))
    return pl.pallas_call(
        _trig_body,
        out_shape=(tab, tab, tab),
        grid=(s // ROW_TILE,),
        in_specs=[
            pl.BlockSpec((ROW_TILE, 1), lambda i: (i, 0)),
            pl.BlockSpec((1, LANES), lambda i: (0, 0)),
        ],
        out_specs=(row, row, row),
        compiler_params=_params("parallel"),
        name="trig_tables",
    )(pos, invf)


def _rope_slab(x, c, sa, sb):
    half = HEAD_DIM // 2
    return x * c + pltpu.roll(x, LANES - half, 1) * sa + pltpu.roll(x, half, 1) * sb


def _kv_layout_body(kv_ref, c_ref, sa_ref, sb_ref, kdup_ref, vlo_ref, vhi_ref):
    c, sa, sb = c_ref[...], sa_ref[...], sb_ref[...]
    lane = lax.broadcasted_iota(jnp.int32, c.shape, 1)
    left = lane < HEAD_DIM
    slab0 = kv_ref[:, 0:LANES]
    slab1 = kv_ref[:, LANES:2 * LANES]
    slab2 = kv_ref[:, 2 * LANES:3 * LANES]
    r0 = _rope_slab(slab0, c, sa, sb)
    r1 = _rope_slab(slab1, c, sa, sb)
    r0s = pltpu.roll(r0, HEAD_DIM, 1)
    r1s = pltpu.roll(r1, HEAD_DIM, 1)
    s1s = pltpu.roll(slab1, HEAD_DIM, 1)
    s2s = pltpu.roll(slab2, HEAD_DIM, 1)
    zero = jnp.zeros_like(c)
    kdup_ref[0] = jnp.where(left, r0, r0s).astype(BF16)
    kdup_ref[1] = jnp.where(left, r0s, r0).astype(BF16)
    kdup_ref[2] = jnp.where(left, r1, r1s).astype(BF16)
    vlo_ref[0] = jnp.where(left, s1s, zero).astype(BF16)
    vhi_ref[0] = jnp.where(left, zero, slab1).astype(BF16)
    vlo_ref[1] = jnp.where(left, slab2, zero).astype(BF16)
    vhi_ref[1] = jnp.where(left, zero, s2s).astype(BF16)
    vlo_ref[2] = jnp.where(left, s2s, zero).astype(BF16)
    vhi_ref[2] = jnp.where(left, zero, slab2).astype(BF16)


def _kv_layout(kv, c, sa, sb):
    s, n = kv.shape
    n_kv = n // (2 * HEAD_DIM)
    assert n == 3 * LANES and n_kv == 3
    out = jax.ShapeDtypeStruct((n_kv, s, LANES), BF16)
    row = pl.BlockSpec((ROW_TILE, LANES), lambda i: (i, 0))
    slab = pl.BlockSpec((n_kv, ROW_TILE, LANES), lambda i: (0, i, 0))
    return pl.pallas_call(
        _kv_layout_body,
        out_shape=(out, out, out),
        grid=(s // ROW_TILE,),
        in_specs=[pl.BlockSpec((ROW_TILE, n), lambda i: (i, 0)), row, row, row],
        out_specs=(slab, slab, slab),
        compiler_params=_params("parallel"),
        name="kv_layout",
    )(kv, c, sa, sb)


def _swa_mix_body(sink_ref, a_ref, c_ref, sa_ref, sb_ref,
                  kp_ref, kc_ref, vlp_ref, vlc_ref, vhp_ref, vhc_ref, mkv_ref,
                  y_ref, qlo_ref, qhi_ref, kf_ref, vlf_ref, vhf_ref, *, mix, mem_width):
    i = pl.program_id(0)
    tm = a_ref.shape[0]
    blk = ATTN_BLOCK
    nb = tm // blk
    n_kv = kc_ref.shape[0]
    pairs_per_kv = GQA_GROUP // 2
    n_pairs = mix // LANES

    kf_ref[:, 0:blk, :] = kp_ref[...]
    kf_ref[:, blk:, :] = kc_ref[...]
    vlf_ref[:, 0:blk, :] = vlp_ref[...]
    vlf_ref[:, blk:, :] = vlc_ref[...]
    vhf_ref[:, 0:blk, :] = vhp_ref[...]
    vhf_ref[:, blk:, :] = vhc_ref[...]

    c, sa, sb = c_ref[...], sa_ref[...], sb_ref[...]
    lane = lax.broadcasted_iota(jnp.int32, c.shape, 1)
    left = lane < HEAD_DIM
    scale = HEAD_DIM ** -0.5
    for p in range(n_pairs):
        sl = slice(p * LANES, (p + 1) * LANES)
        q = _rope_slab(a_ref[:, sl], c, sa, sb) * scale
        qlo_ref[:, sl] = jnp.where(left, q, 0.0).astype(BF16)
        qhi_ref[:, sl] = jnp.where(left, 0.0, q).astype(BF16)

    r = lax.broadcasted_iota(jnp.int32, (blk, 2 * blk), 0)
    kj = lax.broadcasted_iota(jnp.int32, (blk, 2 * blk), 1)
    band = (kj > r) & (kj <= r + blk)

    def block_step(n, carry):
        r0 = pl.multiple_of(n * blk, blk)
        rows = pl.ds(r0, blk)
        keys = pl.ds(r0, 2 * blk)
        first_key = jnp.where(i * nb + n == 0, blk, 0)
        valid = band & (kj >= first_key)
        for kvh in range(n_kv):
            p0 = kvh * pairs_per_kv
            q_all = jnp.concatenate(
                [qlo_ref[rows, (p0 + p) * LANES:(p0 + p + 1) * LANES] for p in range(pairs_per_kv)]
                + [qhi_ref[rows, (p0 + p) * LANES:(p0 + p + 1) * LANES] for p in range(pairs_per_kv)],
                axis=0)
            s_all = lax.dot_general(q_all, kf_ref[kvh, keys, :], (((1,), (1,)), ((), ())),
                                    preferred_element_type=F32)
            probs = []
            for idx in range(GQA_GROUP):
                half, p = divmod(idx, pairs_per_kv)
                sink = sink_ref[2 * (p0 + p) + half]
                s = jnp.where(valid, s_all[idx * blk:(idx + 1) * blk, :], -jnp.inf)
                m = jnp.maximum(jnp.max(s, axis=-1, keepdims=True), sink)
                e = jnp.exp(s - m)
                denom = jnp.sum(e, axis=-1, keepdims=True) + jnp.exp(sink - m)
                probs.append((e / denom).astype(BF16))
            p_lo = jnp.concatenate(probs[:pairs_per_kv], axis=0)
            p_hi = jnp.concatenate(probs[pairs_per_kv:], axis=0)
            o = (jnp.dot(p_lo, vlf_ref[kvh, keys, :], preferred_element_type=F32)
                 + jnp.dot(p_hi, vhf_ref[kvh, keys, :], preferred_element_type=F32))
            for p in range(pairs_per_kv):
                y_ref[rows, (p0 + p) * LANES:(p0 + p + 1) * LANES] = (
                    o[p * blk:(p + 1) * blk, :].astype(y_ref.dtype))
        return carry

    lax.fori_loop(0, nb, block_step, 0)
    _mem_attention(a_ref, mkv_ref, y_ref, mix, mem_width)


def _swa_mix(a, sinks, c, sa, sb, kdup, vlo, vhi, mkv, layer, mix):
    s, d = a.shape
    mem_width = d - mix
    m = mkv.shape[1]
    n_kv = kdup.shape[0]
    nb = ROW_TILE // ATTN_BLOCK
    body = functools.partial(_swa_mix_body, mix=mix, mem_width=mem_width)
    row = pl.BlockSpec((ROW_TILE, LANES), lambda i: (i, 0))
    prev = pl.BlockSpec((n_kv, ATTN_BLOCK, LANES), lambda i: (0, jnp.maximum(i * nb - 1, 0), 0))
    cur = pl.BlockSpec((n_kv, ROW_TILE, LANES), lambda i: (0, i, 0))
    full = pltpu.VMEM((n_kv, ROW_TILE + ATTN_BLOCK, LANES), BF16)
    return pl.pallas_call(
        body,
        out_shape=jax.ShapeDtypeStruct((s, d), BF16),
        grid=(s // ROW_TILE,),
        in_specs=[
            pl.BlockSpec(memory_space=pltpu.SMEM),
            pl.BlockSpec((ROW_TILE, d), lambda i: (i, 0)),
            row, row, row,
            prev, cur, prev, cur, prev, cur,
            pl.BlockSpec((None, m, 2 * mem_width), lambda i: (layer, 0, 0)),
        ],
        out_specs=pl.BlockSpec((ROW_TILE, d), lambda i: (i, 0)),
        scratch_shapes=[
            pltpu.VMEM((ROW_TILE, mix), BF16),
            pltpu.VMEM((ROW_TILE, mix), BF16),
            full, full, full,
        ],
        compiler_params=_params("parallel"),
        name="swa_mix",
    )(sinks, a, c, sa, sb, kdup, kdup, vlo, vlo, vhi, vhi, mkv)


def kernel(x, mem, positions, norms, ffn_gate, ffn_up, ffn_down, w_in, w_out, pool_w, pool_scale,
           w_kv, kv_norm, sinks, w_mem_kv, mem_norm, final_norm):
    batch, seq, d = x.shape
    assert batch == 1
    depth = norms.shape[0]
    n_pool = pool_w.shape[0]
    mix = pool_scale.shape[-1]

    half = HEAD_DIM // 2
    inv_freq = 1.0 / (ROPE_THETA ** (jnp.arange(half, dtype=F32) * (2.0 / HEAD_DIM)))
    invf = jnp.tile(inv_freq, LANES // half)[None, :]
    pos = positions[0].astype(F32)[:, None]
    cos_t, sin_a, sin_b = _trig_tables(pos, invf)

    mkv = _mem_kv(mem[0], mem_norm[None, :], w_mem_kv)

    h = x[0]
    kdup = vlo = vhi = None
    for l in range(depth):
        h = _ffn(h, norms[l, 0][None, :], ffn_gate, ffn_up, ffn_down, l, 0)
        a = _norm_proj(h, norms[l, 1][None, :], w_in, l)
        if l < n_pool:
            y = _pool_mix(a, pool_w, pool_scale, mkv, l, mix)
        else:
            y = _swa_mix(a, sinks[l - n_pool], cos_t, sin_a, sin_b, kdup, vlo, vhi, mkv, l, mix)
        h = _out_proj(y, w_out, h, l)
        last = l == depth - 1
        h = _ffn(h, norms[l, 2][None, :], ffn_gate, ffn_up, ffn_down, l, 1,
                 final_gain=final_norm[None, :] if last else None)
        if l == n_pool - 1:
            kv = _norm_proj(h, kv_norm[None, :], w_kv[None], 0)
            kdup, vlo, vhi = _kv_layout(kv, cos_t, sin_a, sin_b)
    return h[None]
```

```python
import functools

import jax
import jax.numpy as jnp
from jax import lax
from jax.experimental import pallas as pl
from jax.experimental.pallas import tpu as pltpu

F32 = jnp.float32
BF16 = jnp.bfloat16

POOL_WINDOWS = (2, 4, 8, 16)
HEAD_DIM = 64
GQA_GROUP = 8
ATTN_BLOCK = 128
MEM_HEADS = 4
ROPE_THETA = 10000.0
EPS = 1e-5

LANES = 128
MAX_POOL_WINDOW = max(POOL_WINDOWS)

ROW_TILE = 512
FFN_ROW_TILE = 1024
FF_TILE = 512
VMEM_LIMIT = 48 * 1024 * 1024
FFN_VMEM_LIMIT = 56 * 1024 * 1024


def _params(*sem):
    return pltpu.CompilerParams(dimension_semantics=sem, vmem_limit_bytes=VMEM_LIMIT)


def _rms(x, gain):
    y = x * lax.rsqrt(jnp.mean(x * x, axis=-1, keepdims=True) + EPS)
    return y * gain


def _ffn_body(h_hbm, g_ref, wg_ref, wu_ref, wd_ref, fg_ref, o_hbm, acc_ref, xn_ref, in_sem, out_sem,
              *, n_rows, n_ff, final_norm):
    i = pl.program_id(0)
    j = pl.program_id(1)
    tm = acc_ref.shape[1]
    slot = lax.rem(i, 2)
    other = 1 - slot

    def in_copy(tile, s):
        return pltpu.make_async_copy(h_hbm.at[pl.ds(tile * tm, tm), :], acc_ref.at[s], in_sem.at[s])

    def out_copy(tile, s):
        return pltpu.make_async_copy(acc_ref.at[s], o_hbm.at[pl.ds(tile * tm, tm), :], out_sem.at[s])

    @pl.when((i == 0) & (j == 0))
    def _():
        in_copy(0, 0).start()

    @pl.when(j == 0)
    def _():
        in_copy(i, slot).wait()
        xn_ref[...] = _rms(acc_ref[slot], g_ref[...]).astype(BF16)

    @pl.when((j == 1) & (i + 1 < n_rows))
    def _():
        @pl.when(i >= 1)
        def _():
            out_copy(i - 1, other).wait()

        in_copy(i + 1, other).start()

    xn = xn_ref[...]
    g = jnp.dot(xn, wg_ref[...].astype(BF16), preferred_element_type=F32)
    u = jnp.dot(xn, wu_ref[...].astype(BF16), preferred_element_type=F32)
    act = (g / (1.0 + jnp.exp(-g))) * (0.5 * u)
    acc_ref[slot] += jnp.dot(act.astype(BF16), wd_ref[...].astype(BF16), preferred_element_type=F32)

    @pl.when(j == n_ff - 1)
    def _():
        if final_norm:
            acc_ref[slot] = _rms(acc_ref[slot], fg_ref[...])
        out_copy(i, slot).start()

        @pl.when(i == n_rows - 1)
        def _():
            out_copy(i, slot).wait()
            if n_rows >= 2:
                out_copy(i - 1, other).wait()


def _ffn(h, gain, wg, wu, wd, layer, half, final_gain=None):
    s, d = h.shape
    ff = wg.shape[-1]
    n_ff = ff // FF_TILE
    n_rows = s // FFN_ROW_TILE
    assert n_ff >= 2 and s % FFN_ROW_TILE == 0 and ff % FF_TILE == 0
    final_norm = final_gain is not None
    fg = final_gain if final_norm else gain
    body = functools.partial(_ffn_body, n_rows=n_rows, n_ff=n_ff, final_norm=final_norm)
    return pl.pallas_call(
        body,
        out_shape=jax.ShapeDtypeStruct((s, d), F32),
        grid=(n_rows, n_ff),
        in_specs=[
            pl.BlockSpec(memory_space=pl.ANY),
            pl.BlockSpec((1, d), lambda i, j: (0, 0)),
            pl.BlockSpec((None, None, d, FF_TILE), lambda i, j: (layer, half, 0, j)),
            pl.BlockSpec((None, None, d, FF_TILE), lambda i, j: (layer, half, 0, j)),
            pl.BlockSpec((None, None, FF_TILE, d), lambda i, j: (layer, half, j, 0)),
            pl.BlockSpec((1, d), lambda i, j: (0, 0)),
        ],
        out_specs=pl.BlockSpec(memory_space=pl.ANY),
        scratch_shapes=[
            pltpu.VMEM((2, FFN_ROW_TILE, d), F32),
            pltpu.VMEM((FFN_ROW_TILE, d), BF16),
            pltpu.SemaphoreType.DMA((2,)),
            pltpu.SemaphoreType.DMA((2,)),
        ],
        compiler_params=pltpu.CompilerParams(
            dimension_semantics=("arbitrary", "arbitrary"), vmem_limit_bytes=FFN_VMEM_LIMIT),
        name="ffn",
    )(h, gain, wg, wu, wd, fg)


def _norm_proj_body(h_ref, g_ref, w_ref, o_ref):
    xn = _rms(h_ref[...], g_ref[...]).astype(BF16)
    o_ref[...] = jnp.dot(xn, w_ref[...].astype(BF16), preferred_element_type=F32)


def _norm_proj(h, gain, w, layer):
    s, d = h.shape
    n = w.shape[-1]
    return pl.pallas_call(
        _norm_proj_body,
        out_shape=jax.ShapeDtypeStruct((s, n), F32),
        grid=(s // ROW_TILE,),
        in_specs=[
            pl.BlockSpec((ROW_TILE, d), lambda i: (i, 0)),
            pl.BlockSpec((1, d), lambda i: (0, 0)),
            pl.BlockSpec((None, d, n), lambda i: (layer, 0, 0), pipeline_mode=pl.Buffered(1)),
        ],
        out_specs=pl.BlockSpec((ROW_TILE, n), lambda i: (i, 0)),
        compiler_params=_params("parallel"),
        name="norm_proj",
    )(h, gain, w)


def _out_proj_body(y_ref, w_ref, h_ref, o_ref):
    o_ref[...] = h_ref[...] + jnp.dot(y_ref[...], w_ref[...].astype(BF16), preferred_element_type=F32)


def _out_proj(y, w, h, layer):
    s, d = h.shape
    k = y.shape[-1]
    return pl.pallas_call(
        _out_proj_body,
        out_shape=jax.ShapeDtypeStruct((s, d), F32),
        grid=(s // ROW_TILE,),
        in_specs=[
            pl.BlockSpec((ROW_TILE, k), lambda i: (i, 0)),
            pl.BlockSpec((None, k, d), lambda i: (layer, 0, 0), pipeline_mode=pl.Buffered(1)),
            pl.BlockSpec((ROW_TILE, d), lambda i: (i, 0)),
        ],
        out_specs=pl.BlockSpec((ROW_TILE, d), lambda i: (i, 0)),
        compiler_params=_params("parallel"),
        name="out_proj",
    )(y, w, h)


def _mem_kv_body(mem_ref, g_ref, w_ref, o_ref):
    mn = _rms(mem_ref[...], g_ref[...]).astype(BF16)
    o_ref[...] = jnp.dot(mn, w_ref[...].astype(BF16), preferred_element_type=F32).astype(BF16)


def _mem_kv(mem, gain, w):
    depth, d, n = w.shape
    m = mem.shape[0]
    return pl.pallas_call(
        _mem_kv_body,
        out_shape=jax.ShapeDtypeStruct((depth, m, n), BF16),
        grid=(depth,),
        in_specs=[
            pl.BlockSpec((m, d), lambda l: (0, 0)),
            pl.BlockSpec((1, d), lambda l: (0, 0)),
            pl.BlockSpec((None, d, n), lambda l: (l, 0, 0)),
        ],
        out_specs=pl.BlockSpec((None, m, n), lambda l: (l, 0, 0)),
        compiler_params=_params("parallel"),
        name="mem_kv",
    )(mem, gain, w)


def _mem_attention(a_ref, mkv_ref, y_ref, mix, mem_width):
    hd = mem_width // MEM_HEADS
    scale = hd ** -0.5
    scores = []
    for head in range(MEM_HEADS):
        lo = mix + hd * head
        q = a_ref[:, lo:lo + hd].astype(BF16)
        k = mkv_ref[:, hd * head:hd * (head + 1)]
        scores.append(lax.dot_general(q, k, (((1,), (1,)), ((), ())), preferred_element_type=F32) * scale)
    for head in range(MEM_HEADS):
        lo = mix + hd * head
        s = scores[head]
        v = mkv_ref[:, mem_width + hd * head:mem_width + hd * (head + 1)]
        p = jnp.exp(s - jnp.max(s, axis=-1, keepdims=True))
        p = p * (1.0 / jnp.sum(p, axis=-1, keepdims=True))
        o = jnp.dot(p.astype(BF16), v, preferred_element_type=F32)
        y_ref[:, lo:lo + hd] = o.astype(y_ref.dtype)


def _pool_mix_body(a_ref, pw_ref, ps_ref, mkv_ref, y_ref, carry_ref, *, mix, mem_width):
    i = pl.program_id(0)
    tm = a_ref.shape[0]
    cg = mix // len(POOL_WINDOWS)

    @pl.when(i == 0)
    def _():
        carry_ref[...] = jnp.zeros_like(carry_ref)

    t1 = (i * tm + lax.broadcasted_iota(jnp.int32, (tm, 1), 0) + 1).astype(F32)
    for grp, w in enumerate(POOL_WINDOWS):
        sl = slice(grp * cg, (grp + 1) * cg)
        ug = a_ref[:, sl]
        win = jnp.concatenate([carry_ref[:, sl], ug], axis=0)
        shift = 1
        while shift < w:
            win = win + pltpu.roll(win, shift, 0)
            shift *= 2
        pooled = win[MAX_POOL_WINDOW:, :] * (1.0 / jnp.minimum(t1, float(w)))
        diff = (pooled - ug).astype(BF16)
        yg = jnp.dot(diff, pw_ref[grp].astype(BF16), preferred_element_type=F32) * ps_ref[:, sl]
        y_ref[:, sl] = yg.astype(y_ref.dtype)
        carry_ref[:, sl] = ug[tm - MAX_POOL_WINDOW:, :]

    _mem_attention(a_ref, mkv_ref, y_ref, mix, mem_width)


def _pool_mix(a, pool_w, pool_scale, mkv, layer, mix):
    s, d = a.shape
    mem_width = d - mix
    m = mkv.shape[1]
    ng, cg, _ = pool_w.shape[1:]
    body = functools.partial(_pool_mix_body, mix=mix, mem_width=mem_width)
    return pl.pallas_call(
        body,
        out_shape=jax.ShapeDtypeStruct((s, d), BF16),
        grid=(s // ROW_TILE,),
        in_specs=[
            pl.BlockSpec((ROW_TILE, d), lambda i: (i, 0)),
            pl.BlockSpec((None, ng, cg, cg), lambda i: (layer, 0, 0, 0)),
            pl.BlockSpec((None, 1, mix), lambda i: (layer, 0, 0)),
            pl.BlockSpec((None, m, 2 * mem_width), lambda i: (layer, 0, 0)),
        ],
        out_specs=pl.BlockSpec((ROW_TILE, d), lambda i: (i, 0)),
        scratch_shapes=[pltpu.VMEM((MAX_POOL_WINDOW, mix), F32)],
        compiler_params=_params("arbitrary"),
        name="pool_mix",
    )(a, pool_w, pool_scale[:, None, :], mkv)


def _trig_body(pos_ref, invf_ref, c_ref, sa_ref, sb_ref):
    ang = pos_ref[...] * invf_ref[...]
    c = jnp.cos(ang)
    s = jnp.sin(ang)
    lane = lax.broadcasted_iota(jnp.int32, ang.shape, 1)
    first_half = (lane & (HEAD_DIM // 2)) == 0
    c_ref[...] = c
    sa_ref[...] = jnp.where(first_half, -s, 0.0)
    sb_ref[...] = jnp.where(first_half, 0.0, s)


def _trig_tables(pos, invf):
    s = pos.shape[0]
    tab = jax.ShapeDtypeStruct((s, LANES), F32)
    row = pl.BlockSpec((ROW_TILE, LANES), lambda i: (i, 0))
    return pl.pallas_call(
        _trig_body,
        out_shape=(tab, tab, tab),
        grid=(s // ROW_TILE,),
        in_specs=[
            pl.BlockSpec((ROW_TILE, 1), lambda i: (i, 0)),
            pl.BlockSpec((1, LANES), lambda i: (0, 0)),
        ],
        out_specs=(row, row, row),
        compiler_params=_params("parallel"),
        name="trig_tables",
    )(pos, invf)


def _rope_slab(x, c, sa, sb):
    half = HEAD_DIM // 2
    return x * c + pltpu.roll(x, LANES - half, 1) * sa + pltpu.roll(x, half, 1) * sb


def _kv_layout_body(kv_ref, c_ref, sa_ref, sb_ref, kdup_ref, vlo_ref, vhi_ref):
    c, sa, sb = c_ref[...], sa_ref[...], sb_ref[...]
    lane = lax.broadcasted_iota(jnp.int32, c.shape, 1)
    left = lane < HEAD_DIM
    slab0 = kv_ref[:, 0:LANES]
    slab1 = kv_ref[:, LANES:2 * LANES]
    slab2 = kv_ref[:, 2 * LANES:3 * LANES]
    r0 = _rope_slab(slab0, c, sa, sb)
    r1 = _rope_slab(slab1, c, sa, sb)
    r0s = pltpu.roll(r0, HEAD_DIM, 1)
    r1s = pltpu.roll(r1, HEAD_DIM, 1)
    s1s = pltpu.roll(slab1, HEAD_DIM, 1)
    s2s = pltpu.roll(slab2, HEAD_DIM, 1)
    zero = jnp.zeros_like(c)
    kdup_ref[0] = jnp.where(left, r0, r0s).astype(BF16)
    kdup_ref[1] = jnp.where(left, r0s, r0).astype(BF16)
    kdup_ref[2] = jnp.where(left, r1, r1s).astype(BF16)
    vlo_ref[0] = jnp.where(left, s1s, zero).astype(BF16)
    vhi_ref[0] = jnp.where(left, zero, slab1).astype(BF16)
    vlo_ref[1] = jnp.where(left, slab2, zero).astype(BF16)
    vhi_ref[1] = jnp.where(left, zero, s2s).astype(BF16)
    vlo_ref[2] = jnp.where(left, s2s, zero).astype(BF16)
    vhi_ref[2] = jnp.where(left, zero, slab2).astype(BF16)


def _kv_layout(kv, c, sa, sb):
    s, n = kv.shape
    n_kv = n // (2 * HEAD_DIM)
    assert n == 3 * LANES and n_kv == 3
    out = jax.ShapeDtypeStruct((n_kv, s, LANES), BF16)
    row = pl.BlockSpec((ROW_TILE, LANES), lambda i: (i, 0))
    slab = pl.BlockSpec((n_kv, ROW_TILE, LANES), lambda i: (0, i, 0))
    return pl.pallas_call(
        _kv_layout_body,
        out_shape=(out, out, out),
        grid=(s // ROW_TILE,),
        in_specs=[pl.BlockSpec((ROW_TILE, n), lambda i: (i, 0)), row, row, row],
        out_specs=(slab, slab, slab),
        compiler_params=_params("parallel"),
        name="kv_layout",
    )(kv, c, sa, sb)


def _swa_mix_body(sink_ref, a_ref, c_ref, sa_ref, sb_ref,
                  kp_ref, kc_ref, vlp_ref, vlc_ref, vhp_ref, vhc_ref, mkv_ref,
                  y_ref, qlo_ref, qhi_ref, kf_ref, vlf_ref, vhf_ref, *, mix, mem_width):
    i = pl.program_id(0)
    tm = a_ref.shape[0]
    blk = ATTN_BLOCK
    nb = tm // blk
    n_kv = kc_ref.shape[0]
    pairs_per_kv = GQA_GROUP // 2
    n_pairs = mix // LANES

    kf_ref[:, 0:blk, :] = kp_ref[...]
    kf_ref[:, blk:, :] = kc_ref[...]
    vlf_ref[:, 0:blk, :] = vlp_ref[...]
    vlf_ref[:, blk:, :] = vlc_ref[...]
    vhf_ref[:, 0:blk, :] = vhp_ref[...]
    vhf_ref[:, blk:, :] = vhc_ref[...]

    c, sa, sb = c_ref[...], sa_ref[...], sb_ref[...]
    lane = lax.broadcasted_iota(jnp.int32, c.shape, 1)
    left = lane < HEAD_DIM
    scale = HEAD_DIM ** -0.5
    for p in range(n_pairs):
        sl = slice(p * LANES, (p + 1) * LANES)
        q = _rope_slab(a_ref[:, sl], c, sa, sb) * scale
        qlo_ref[:, sl] = jnp.where(left, q, 0.0).astype(BF16)
        qhi_ref[:, sl] = jnp.where(left, 0.0, q).astype(BF16)

    own = (lax.broadcasted_iota(jnp.int32, (blk, blk), 1)
           <= lax.broadcasted_iota(jnp.int32, (blk, blk), 0))

    def block_step(n, carry):
        r0 = pl.multiple_of(n * blk, blk)
        rows = pl.ds(r0, blk)
        keys = pl.ds(r0, 2 * blk)
        before_bias = jnp.where(i * nb + n == 0, -jnp.inf, 0.0)
        scores = []
        for kvh in range(n_kv):
            p0 = kvh * pairs_per_kv
            q_all = jnp.concatenate(
                [qlo_ref[rows, (p0 + p) * LANES:(p0 + p + 1) * LANES] for p in range(pairs_per_kv)]
                + [qhi_ref[rows, (p0 + p) * LANES:(p0 + p + 1) * LANES] for p in range(pairs_per_kv)],
                axis=0)
            scores.append(lax.dot_general(q_all, kf_ref[kvh, keys, :], (((1,), (1,)), ((), ())),
                                          preferred_element_type=F32))
        for kvh in range(n_kv):
            p0 = kvh * pairs_per_kv
            s_all = scores[kvh]
            probs = []
            for idx in range(GQA_GROUP):
                half, p = divmod(idx, pairs_per_kv)
                sink = sink_ref[2 * (p0 + p) + half]
                s_head = s_all[idx * blk:(idx + 1) * blk, :]
                s = jnp.where(own, s_head[:, blk:], s_head[:, :blk] + before_bias)
                m = jnp.maximum(jnp.max(s, axis=-1, keepdims=True), sink)
                e = jnp.exp(s - m)
                denom = jnp.sum(e, axis=-1, keepdims=True) + jnp.exp(sink - m)
                pr = e * (1.0 / denom)
                probs.append(jnp.concatenate(
                    [jnp.where(own, 0.0, pr), jnp.where(own, pr, 0.0)], axis=-1).astype(BF16))
            p_lo = jnp.concatenate(probs[:pairs_per_kv], axis=0)
            p_hi = jnp.concatenate(probs[pairs_per_kv:], axis=0)
            o = (jnp.dot(p_lo, vlf_ref[kvh, keys, :], preferred_element_type=F32)
                 + jnp.dot(p_hi, vhf_ref[kvh, keys, :], preferred_element_type=F32))
            for p in range(pairs_per_kv):
                y_ref[rows, (p0 + p) * LANES:(p0 + p + 1) * LANES] = (
                    o[p * blk:(p + 1) * blk, :].astype(y_ref.dtype))
        return carry

    lax.fori_loop(0, nb, block_step, 0, unroll=2)
    _mem_attention(a_ref, mkv_ref, y_ref, mix, mem_width)


def _swa_mix(a, sinks, c, sa, sb, kdup, vlo, vhi, mkv, layer, mix):
    s, d = a.shape
    mem_width = d - mix
    m = mkv.shape[1]
    n_kv = kdup.shape[0]
    nb = ROW_TILE // ATTN_BLOCK
    body = functools.partial(_swa_mix_body, mix=mix, mem_width=mem_width)
    row = pl.BlockSpec((ROW_TILE, LANES), lambda i: (i, 0))
    prev = pl.BlockSpec((n_kv, ATTN_BLOCK, LANES), lambda i: (0, jnp.maximum(i * nb - 1, 0), 0))
    cur = pl.BlockSpec((n_kv, ROW_TILE, LANES), lambda i: (0, i, 0))
    full = pltpu.VMEM((n_kv, ROW_TILE + ATTN_BLOCK, LANES), BF16)
    return pl.pallas_call(
        body,
        out_shape=jax.ShapeDtypeStruct((s, d), BF16),
        grid=(s // ROW_TILE,),
        in_specs=[
            pl.BlockSpec(memory_space=pltpu.SMEM),
            pl.BlockSpec((ROW_TILE, d), lambda i: (i, 0)),
            row, row, row,
            prev, cur, prev, cur, prev, cur,
            pl.BlockSpec((None, m, 2 * mem_width), lambda i: (layer, 0, 0)),
        ],
        out_specs=pl.BlockSpec((ROW_TILE, d), lambda i: (i, 0)),
        scratch_shapes=[
            pltpu.VMEM((ROW_TILE, mix), BF16),
            pltpu.VMEM((ROW_TILE, mix), BF16),
            full, full, full,
        ],
        compiler_params=_params("parallel"),
        name="swa_mix",
    )(sinks, a, c, sa, sb, kdup, kdup, vlo, vlo, vhi, vhi, mkv)


def kernel(x, mem, positions, norms, ffn_gate, ffn_up, ffn_down, w_in, w_out, pool_w, pool_scale,
           w_kv, kv_norm, sinks, w_mem_kv, mem_norm, final_norm):
    batch, seq, d = x.shape
    assert batch == 1
    depth = norms.shape[0]
    n_pool = pool_w.shape[0]
    mix = pool_scale.shape[-1]

    half = HEAD_DIM // 2
    inv_freq = 1.0 / (ROPE_THETA ** (jnp.arange(half, dtype=F32) * (2.0 / HEAD_DIM)))
    invf = jnp.tile(inv_freq, LANES // half)[None, :]
    pos = positions[0].astype(F32)[:, None]
    cos_t, sin_a, sin_b = _trig_tables(pos, invf)

    mkv = _mem_kv(mem[0], mem_norm[None, :], w_mem_kv)

    h = x[0]
    kdup = vlo = vhi = None
    for l in range(depth):
        h = _ffn(h, norms[l, 0][None, :], ffn_gate, ffn_up, ffn_down, l, 0)
        a = _norm_proj(h, norms[l, 1][None, :], w_in, l)
        if l < n_pool:
            y = _pool_mix(a, pool_w, pool_scale, mkv, l, mix)
        else:
            y = _swa_mix(a, sinks[l - n_pool], cos_t, sin_a, sin_b, kdup, vlo, vhi, mkv, l, mix)
        h = _out_proj(y, w_out, h, l)
        last = l == depth - 1
        h = _ffn(h, norms[l, 2][None, :], ffn_gate, ffn_up, ffn_down, l, 1,
                 final_gain=final_norm[None, :] if last else None)
        if l == n_pool - 1:
            kv = _norm_proj(h, kv_norm[None, :], w_kv[None], 0)
            kdup, vlo, vhi = _kv_layout(kv, cos_t, sin_a, sin_b)
    return h[None]
```

```python
import functools

import jax
import jax.numpy as jnp
from jax import lax
from jax.experimental import pallas as pl
from jax.experimental.pallas import tpu as pltpu

F32 = jnp.float32
BF16 = jnp.bfloat16

POOL_WINDOWS = (2, 4, 8, 16)
HEAD_DIM = 64
GQA_GROUP = 8
ATTN_BLOCK = 128
MEM_HEADS = 4
ROPE_THETA = 10000.0
EPS = 1e-5

LANES = 128
MAX_POOL_WINDOW = max(POOL_WINDOWS)

ROW_TILE = 512
FFN_ROW_TILE = 1024
FF_TILE = 512
VMEM_LIMIT = 48 * 1024 * 1024
FFN_VMEM_LIMIT = 56 * 1024 * 1024
LAYER_VMEM_LIMIT = 58 * 1024 * 1024


def _params(*sem):
    return pltpu.CompilerParams(dimension_semantics=sem, vmem_limit_bytes=VMEM_LIMIT)


def _layer_params():
    return pltpu.CompilerParams(dimension_semantics=("arbitrary",), vmem_limit_bytes=LAYER_VMEM_LIMIT)


def _rms(x, gain):
    y = x * lax.rsqrt(jnp.mean(x * x, axis=-1, keepdims=True) + EPS)
    return y * gain


def _ffn_body(h_hbm, g_ref, wg_ref, wu_ref, wd_ref, fg_ref, o_hbm, acc_ref, xn_ref, in_sem, out_sem,
              *, n_rows, n_ff, final_norm):
    i = pl.program_id(0)
    j = pl.program_id(1)
    tm = acc_ref.shape[1]
    slot = lax.rem(i, 2)
    other = 1 - slot

    def in_copy(tile, s):
        return pltpu.make_async_copy(h_hbm.at[pl.ds(tile * tm, tm), :], acc_ref.at[s], in_sem.at[s])

    def out_copy(tile, s):
        return pltpu.make_async_copy(acc_ref.at[s], o_hbm.at[pl.ds(tile * tm, tm), :], out_sem.at[s])

    @pl.when((i == 0) & (j == 0))
    def _():
        in_copy(0, 0).start()

    @pl.when(j == 0)
    def _():
        in_copy(i, slot).wait()
        xn_ref[...] = _rms(acc_ref[slot], g_ref[...]).astype(BF16)

    @pl.when((j == 1) & (i + 1 < n_rows))
    def _():
        @pl.when(i >= 1)
        def _():
            out_copy(i - 1, other).wait()

        in_copy(i + 1, other).start()

    xn = xn_ref[...]
    g = jnp.dot(xn, wg_ref[...].astype(BF16), preferred_element_type=F32)
    u = jnp.dot(xn, wu_ref[...].astype(BF16), preferred_element_type=F32)
    act = (g / (1.0 + jnp.exp(-g))) * (0.5 * u)
    acc_ref[slot] += jnp.dot(act.astype(BF16), wd_ref[...].astype(BF16), preferred_element_type=F32)

    @pl.when(j == n_ff - 1)
    def _():
        if final_norm:
            acc_ref[slot] = _rms(acc_ref[slot], fg_ref[...])
        out_copy(i, slot).start()

        @pl.when(i == n_rows - 1)
        def _():
            out_copy(i, slot).wait()
            if n_rows >= 2:
                out_copy(i - 1, other).wait()


def _ffn(h, gain, wg, wu, wd, layer, half, final_gain=None):
    s, d = h.shape
    ff = wg.shape[-1]
    n_ff = ff // FF_TILE
    n_rows = s // FFN_ROW_TILE
    assert n_ff >= 2 and s % FFN_ROW_TILE == 0 and ff % FF_TILE == 0
    final_norm = final_gain is not None
    fg = final_gain if final_norm else gain
    body = functools.partial(_ffn_body, n_rows=n_rows, n_ff=n_ff, final_norm=final_norm)
    return pl.pallas_call(
        body,
        out_shape=jax.ShapeDtypeStruct((s, d), F32),
        grid=(n_rows, n_ff),
        in_specs=[
            pl.BlockSpec(memory_space=pl.ANY),
            pl.BlockSpec((1, d), lambda i, j: (0, 0)),
            pl.BlockSpec((None, None, d, FF_TILE), lambda i, j: (layer, half, 0, j)),
            pl.BlockSpec((None, None, d, FF_TILE), lambda i, j: (layer, half, 0, j)),
            pl.BlockSpec((None, None, FF_TILE, d), lambda i, j: (layer, half, j, 0)),
            pl.BlockSpec((1, d), lambda i, j: (0, 0)),
        ],
        out_specs=pl.BlockSpec(memory_space=pl.ANY),
        scratch_shapes=[
            pltpu.VMEM((2, FFN_ROW_TILE, d), F32),
            pltpu.VMEM((FFN_ROW_TILE, d), BF16),
            pltpu.SemaphoreType.DMA((2,)),
            pltpu.SemaphoreType.DMA((2,)),
        ],
        compiler_params=pltpu.CompilerParams(
            dimension_semantics=("arbitrary", "arbitrary"), vmem_limit_bytes=FFN_VMEM_LIMIT),
        name="ffn",
    )(h, gain, wg, wu, wd, fg)


def _norm_proj_body(h_ref, g_ref, w_ref, o_ref):
    xn = _rms(h_ref[...], g_ref[...]).astype(BF16)
    o_ref[...] = jnp.dot(xn, w_ref[...].astype(BF16), preferred_element_type=F32)


def _norm_proj(h, gain, w, layer):
    s, d = h.shape
    n = w.shape[-1]
    return pl.pallas_call(
        _norm_proj_body,
        out_shape=jax.ShapeDtypeStruct((s, n), F32),
        grid=(s // ROW_TILE,),
        in_specs=[
            pl.BlockSpec((ROW_TILE, d), lambda i: (i, 0)),
            pl.BlockSpec((1, d), lambda i: (0, 0)),
            pl.BlockSpec((None, d, n), lambda i: (layer, 0, 0), pipeline_mode=pl.Buffered(1)),
        ],
        out_specs=pl.BlockSpec((ROW_TILE, n), lambda i: (i, 0)),
        compiler_params=_params("parallel"),
        name="norm_proj",
    )(h, gain, w)


def _mem_kv_body(mem_ref, g_ref, w_ref, o_ref):
    mn = _rms(mem_ref[...], g_ref[...]).astype(BF16)
    o_ref[...] = jnp.dot(mn, w_ref[...].astype(BF16), preferred_element_type=F32).astype(BF16)


def _mem_kv(mem, gain, w):
    depth, d, n = w.shape
    m = mem.shape[0]
    return pl.pallas_call(
        _mem_kv_body,
        out_shape=jax.ShapeDtypeStruct((depth, m, n), BF16),
        grid=(depth,),
        in_specs=[
            pl.BlockSpec((m, d), lambda l: (0, 0)),
            pl.BlockSpec((1, d), lambda l: (0, 0)),
            pl.BlockSpec((None, d, n), lambda l: (l, 0, 0)),
        ],
        out_specs=pl.BlockSpec((None, m, n), lambda l: (l, 0, 0)),
        compiler_params=_params("parallel"),
        name="mem_kv",
    )(mem, gain, w)


def _mem_scores(a_ref, mkv_ref, mix, mem_width):
    hd = mem_width // MEM_HEADS
    scale = hd ** -0.5
    scores = []
    for head in range(MEM_HEADS):
        lo = mix + hd * head
        q = a_ref[:, lo:lo + hd].astype(BF16)
        k = mkv_ref[:, hd * head:hd * (head + 1)]
        scores.append(lax.dot_general(q, k, (((1,), (1,)), ((), ())), preferred_element_type=F32) * scale)
    return scores


def _mem_finish(scores, mkv_ref, y_ref, mix, mem_width):
    hd = mem_width // MEM_HEADS
    for head in range(MEM_HEADS):
        lo = mix + hd * head
        s = scores[head]
        v = mkv_ref[:, mem_width + hd * head:mem_width + hd * (head + 1)]
        p = jnp.exp(s - jnp.max(s, axis=-1, keepdims=True))
        p = p * (1.0 / jnp.sum(p, axis=-1, keepdims=True))
        o = jnp.dot(p.astype(BF16), v, preferred_element_type=F32)
        y_ref[:, lo:lo + hd] = o.astype(y_ref.dtype)


def _out_proj_prev(y_ref, wo_ref, h_ref, o_ref):
    o_ref[...] = h_ref[...] + jnp.dot(y_ref[...], wo_ref[...].astype(BF16), preferred_element_type=F32)


def _pool_mix_body(a_ref, pw_ref, ps_ref, mkv_ref, wo_ref, h_ref, o_ref, carry_ref, y_ref, *, mix, mem_width):
    i = pl.program_id(0)
    tm = a_ref.shape[0]
    cg = mix // len(POOL_WINDOWS)

    @pl.when(i == 0)
    def _():
        carry_ref[...] = jnp.zeros_like(carry_ref)
        y_ref[...] = jnp.zeros_like(y_ref)

    mem_scores = _mem_scores(a_ref, mkv_ref, mix, mem_width)
    _out_proj_prev(y_ref, wo_ref, h_ref, o_ref)

    t1 = (i * tm + lax.broadcasted_iota(jnp.int32, (tm, 1), 0) + 1).astype(F32)
    for grp, w in enumerate(POOL_WINDOWS):
        sl = slice(grp * cg, (grp + 1) * cg)
        ug = a_ref[:, sl]
        win = jnp.concatenate([carry_ref[:, sl], ug], axis=0)
        shift = 1
        while shift < w:
            win = win + pltpu.roll(win, shift, 0)
            shift *= 2
        pooled = win[MAX_POOL_WINDOW:, :] * (1.0 / jnp.minimum(t1, float(w)))
        diff = (pooled - ug).astype(BF16)
        yg = jnp.dot(diff, pw_ref[grp].astype(BF16), preferred_element_type=F32) * ps_ref[:, sl]
        y_ref[:, sl] = yg.astype(y_ref.dtype)
        carry_ref[:, sl] = ug[tm - MAX_POOL_WINDOW:, :]

    _mem_finish(mem_scores, mkv_ref, y_ref, mix, mem_width)


def _pool_layer(a, pool_w, pool_scale, mkv, w_out, h, layer, mix):
    s, d = a.shape
    mem_width = d - mix
    m = mkv.shape[1]
    ng, cg, _ = pool_w.shape[1:]
    n_tiles = s // ROW_TILE
    body = functools.partial(_pool_mix_body, mix=mix, mem_width=mem_width)
    return pl.pallas_call(
        body,
        out_shape=jax.ShapeDtypeStruct((s, d), F32),
        grid=(n_tiles + 1,),
        in_specs=[
            pl.BlockSpec((ROW_TILE, d), lambda i: (jnp.minimum(i, n_tiles - 1), 0)),
            pl.BlockSpec((None, ng, cg, cg), lambda i: (layer, 0, 0, 0)),
            pl.BlockSpec((None, 1, mix), lambda i: (layer, 0, 0)),
            pl.BlockSpec((None, m, 2 * mem_width), lambda i: (layer, 0, 0)),
            pl.BlockSpec((None, d, d), lambda i: (layer, 0, 0), pipeline_mode=pl.Buffered(1)),
            pl.BlockSpec((ROW_TILE, d), lambda i: (jnp.maximum(i - 1, 0), 0)),
        ],
        out_specs=pl.BlockSpec((ROW_TILE, d), lambda i: (jnp.maximum(i - 1, 0), 0)),
        scratch_shapes=[pltpu.VMEM((MAX_POOL_WINDOW, mix), F32), pltpu.VMEM((ROW_TILE, d), BF16)],
        compiler_params=_layer_params(),
        name="pool_layer",
    )(a, pool_w, pool_scale[:, None, :], mkv, w_out, h)


def _trig_body(pos_ref, invf_ref, c_ref, sa_ref, sb_ref):
    ang = pos_ref[...] * invf_ref[...]
    c = jnp.cos(ang)
    s = jnp.sin(ang)
    lane = lax.broadcasted_iota(jnp.int32, ang.shape, 1)
    first_half = (lane & (HEAD_DIM // 2)) == 0
    c_ref[...] = c
    sa_ref[...] = jnp.where(first_half, -s, 0.0)
    sb_ref[...] = jnp.where(first_half, 0.0, s)


def _trig_tables(pos, invf):
    s = pos.shape[0]
    tab = jax.ShapeDtypeStruct((s, LANES), F32)
    row = pl.BlockSpec((ROW_TILE, LANES), lambda i: (i, 0))
    return pl.pallas_call(
        _trig_body,
        out_shape=(tab, tab, tab),
        grid=(s // ROW_TILE,),
        in_specs=[
            pl.BlockSpec((ROW_TILE, 1), lambda i: (i, 0)),
            pl.BlockSpec((1, LANES), lambda i: (0, 0)),
        ],
        out_specs=(row, row, row),
        compiler_params=_params("parallel"),
        name="trig_tables",
    )(pos, invf)


def _rope_slab(x, c, sa, sb):
    half = HEAD_DIM // 2
    return x * c + pltpu.roll(x, LANES - half, 1) * sa + pltpu.roll(x, half, 1) * sb


def _kv_layout_body(kv_ref, c_ref, sa_ref, sb_ref, kdup_ref, vlo_ref, vhi_ref):
    c, sa, sb = c_ref[...], sa_ref[...], sb_ref[...]
    lane = lax.broadcasted_iota(jnp.int32, c.shape, 1)
    left = lane < HEAD_DIM
    slab0 = kv_ref[:, 0:LANES]
    slab1 = kv_ref[:, LANES:2 * LANES]
    slab2 = kv_ref[:, 2 * LANES:3 * LANES]
    r0 = _rope_slab(slab0, c, sa, sb)
    r1 = _rope_slab(slab1, c, sa, sb)
    r0s = pltpu.roll(r0, HEAD_DIM, 1)
    r1s = pltpu.roll(r1, HEAD_DIM, 1)
    s1s = pltpu.roll(slab1, HEAD_DIM, 1)
    s2s = pltpu.roll(slab2, HEAD_DIM, 1)
    zero = jnp.zeros_like(c)
    kdup_ref[0] = jnp.where(left, r0, r0s).astype(BF16)
    kdup_ref[1] = jnp.where(left, r0s, r0).astype(BF16)
    kdup_ref[2] = jnp.where(left, r1, r1s).astype(BF16)
    vlo_ref[0] = jnp.where(left, s1s, zero).astype(BF16)
    vhi_ref[0] = jnp.where(left, zero, slab1).astype(BF16)
    vlo_ref[1] = jnp.where(left, slab2, zero).astype(BF16)
    vhi_ref[1] = jnp.where(left, zero, s2s).astype(BF16)
    vlo_ref[2] = jnp.where(left, s2s, zero).astype(BF16)
    vhi_ref[2] = jnp.where(left, zero, slab2).astype(BF16)


def _kv_layout(kv, c, sa, sb):
    s, n = kv.shape
    n_kv = n // (2 * HEAD_DIM)
    assert n == 3 * LANES and n_kv == 3
    out = jax.ShapeDtypeStruct((n_kv, s, LANES), BF16)
    row = pl.BlockSpec((ROW_TILE, LANES), lambda i: (i, 0))
    slab = pl.BlockSpec((n_kv, ROW_TILE, LANES), lambda i: (0, i, 0))
    return pl.pallas_call(
        _kv_layout_body,
        out_shape=(out, out, out),
        grid=(s // ROW_TILE,),
        in_specs=[pl.BlockSpec((ROW_TILE, n), lambda i: (i, 0)), row, row, row],
        out_specs=(slab, slab, slab),
        compiler_params=_params("parallel"),
        name="kv_layout",
    )(kv, c, sa, sb)


def _swa_mix_body(sink_ref, a_ref, c_ref, sa_ref, sb_ref,
                  kp_ref, kc_ref, vlp_ref, vlc_ref, vhp_ref, vhc_ref, mkv_ref, wo_ref, h_ref,
                  o_ref, y_ref, qlo_ref, qhi_ref, kf_ref, vlf_ref, vhf_ref, *, mix, mem_width):
    i = pl.program_id(0)

    @pl.when(i == 0)
    def _():
        y_ref[...] = jnp.zeros_like(y_ref)

    mem_scores = _mem_scores(a_ref, mkv_ref, mix, mem_width)
    _out_proj_prev(y_ref, wo_ref, h_ref, o_ref)
    tm = a_ref.shape[0]
    blk = ATTN_BLOCK
    nb = tm // blk
    n_kv = kc_ref.shape[0]
    pairs_per_kv = GQA_GROUP // 2
    n_pairs = mix // LANES

    kf_ref[:, 0:blk, :] = kp_ref[...]
    kf_ref[:, blk:, :] = kc_ref[...]
    vlf_ref[:, 0:blk, :] = vlp_ref[...]
    vlf_ref[:, blk:, :] = vlc_ref[...]
    vhf_ref[:, 0:blk, :] = vhp_ref[...]
    vhf_ref[:, blk:, :] = vhc_ref[...]

    c, sa, sb = c_ref[...], sa_ref[...], sb_ref[...]
    lane = lax.broadcasted_iota(jnp.int32, c.shape, 1)
    left = lane < HEAD_DIM
    scale = HEAD_DIM ** -0.5
    for p in range(n_pairs):
        sl = slice(p * LANES, (p + 1) * LANES)
        q = _rope_slab(a_ref[:, sl], c, sa, sb) * scale
        qlo_ref[:, sl] = jnp.where(left, q, 0.0).astype(BF16)
        qhi_ref[:, sl] = jnp.where(left, 0.0, q).astype(BF16)

    own = (lax.broadcasted_iota(jnp.int32, (blk, blk), 1)
           <= lax.broadcasted_iota(jnp.int32, (blk, blk), 0))

    def block_step(n, carry):
        r0 = pl.multiple_of(n * blk, blk)
        rows = pl.ds(r0, blk)
        keys = pl.ds(r0, 2 * blk)
        before_bias = jnp.where(i * nb + n == 0, -jnp.inf, 0.0)
        scores = []
        for kvh in range(n_kv):
            p0 = kvh * pairs_per_kv
            q_all = jnp.concatenate(
                [qlo_ref[rows, (p0 + p) * LANES:(p0 + p + 1) * LANES] for p in range(pairs_per_kv)]
                + [qhi_ref[rows, (p0 + p) * LANES:(p0 + p + 1) * LANES] for p in range(pairs_per_kv)],
                axis=0)
            scores.append(lax.dot_general(q_all, kf_ref[kvh, keys, :], (((1,), (1,)), ((), ())),
                                          preferred_element_type=F32))
        for kvh in range(n_kv):
            p0 = kvh * pairs_per_kv
            s_all = scores[kvh]
            probs = []
            for idx in range(GQA_GROUP):
                half, p = divmod(idx, pairs_per_kv)
                sink = sink_ref[2 * (p0 + p) + half]
                s_head = s_all[idx * blk:(idx + 1) * blk, :]
                s = jnp.where(own, s_head[:, blk:], s_head[:, :blk] + before_bias)
                m = jnp.maximum(jnp.max(s, axis=-1, keepdims=True), sink)
                e = jnp.exp(s - m)
                denom = jnp.sum(e, axis=-1, keepdims=True) + jnp.exp(sink - m)
                pr = e * (1.0 / denom)
                probs.append(jnp.concatenate(
                    [jnp.where(own, 0.0, pr), jnp.where(own, pr, 0.0)], axis=-1).astype(BF16))
            p_lo = jnp.concatenate(probs[:pairs_per_kv], axis=0)
            p_hi = jnp.concatenate(probs[pairs_per_kv:], axis=0)
            o = (jnp.dot(p_lo, vlf_ref[kvh, keys, :], preferred_element_type=F32)
                 + jnp.dot(p_hi, vhf_ref[kvh, keys, :], preferred_element_type=F32))
            for p in range(pairs_per_kv):
                y_ref[rows, (p0 + p) * LANES:(p0 + p + 1) * LANES] = (
                    o[p * blk:(p + 1) * blk, :].astype(y_ref.dtype))
        return carry

    _mem_finish(mem_scores, mkv_ref, y_ref, mix, mem_width)
    lax.fori_loop(0, nb, block_step, 0, unroll=2)


def _swa_layer(a, sinks, c, sa, sb, kdup, vlo, vhi, mkv, w_out, h, layer, mix):
    s, d = a.shape
    mem_width = d - mix
    m = mkv.shape[1]
    n_kv = kdup.shape[0]
    nb = ROW_TILE // ATTN_BLOCK
    n_tiles = s // ROW_TILE
    body = functools.partial(_swa_mix_body, mix=mix, mem_width=mem_width)

    def tile(i):
        return jnp.minimum(i, n_tiles - 1)

    row = pl.BlockSpec((ROW_TILE, LANES), lambda i: (tile(i), 0))
    prev = pl.BlockSpec((n_kv, ATTN_BLOCK, LANES), lambda i: (0, jnp.maximum(tile(i) * nb - 1, 0), 0))
    cur = pl.BlockSpec((n_kv, ROW_TILE, LANES), lambda i: (0, tile(i), 0))
    full = pltpu.VMEM((n_kv, ROW_TILE + ATTN_BLOCK, LANES), BF16)
    return pl.pallas_call(
        body,
        out_shape=jax.ShapeDtypeStruct((s, d), F32),
        grid=(n_tiles + 1,),
        in_specs=[
            pl.BlockSpec(memory_space=pltpu.SMEM),
            pl.BlockSpec((ROW_TILE, d), lambda i: (tile(i), 0)),
            row, row, row,
            prev, cur, prev, cur, prev, cur,
            pl.BlockSpec((None, m, 2 * mem_width), lambda i: (layer, 0, 0)),
            pl.BlockSpec((None, d, d), lambda i: (layer, 0, 0), pipeline_mode=pl.Buffered(1)),
            pl.BlockSpec((ROW_TILE, d), lambda i: (jnp.maximum(i - 1, 0), 0)),
        ],
        out_specs=pl.BlockSpec((ROW_TILE, d), lambda i: (jnp.maximum(i - 1, 0), 0)),
        scratch_shapes=[
            pltpu.VMEM((ROW_TILE, d), BF16),
            pltpu.VMEM((ROW_TILE, mix), BF16),
            pltpu.VMEM((ROW_TILE, mix), BF16),
            full, full, full,
        ],
        compiler_params=_layer_params(),
        name="swa_layer",
    )(sinks, a, c, sa, sb, kdup, kdup, vlo, vlo, vhi, vhi, mkv, w_out, h)


def kernel(x, mem, positions, norms, ffn_gate, ffn_up, ffn_down, w_in, w_out, pool_w, pool_scale,
           w_kv, kv_norm, sinks, w_mem_kv, mem_norm, final_norm):
    batch, seq, d = x.shape
    assert batch == 1
    depth = norms.shape[0]
    n_pool = pool_w.shape[0]
    mix = pool_scale.shape[-1]

    half = HEAD_DIM // 2
    inv_freq = 1.0 / (ROPE_THETA ** (jnp.arange(half, dtype=F32) * (2.0 / HEAD_DIM)))
    invf = jnp.tile(inv_freq, LANES // half)[None, :]
    pos = positions[0].astype(F32)[:, None]
    cos_t, sin_a, sin_b = _trig_tables(pos, invf)

    mkv = _mem_kv(mem[0], mem_norm[None, :], w_mem_kv)

    h = x[0]
    kdup = vlo = vhi = None
    for l in range(depth):
        h = _ffn(h, norms[l, 0][None, :], ffn_gate, ffn_up, ffn_down, l, 0)
        a = _norm_proj(h, norms[l, 1][None, :], w_in, l)
        if l < n_pool:
            h = _pool_layer(a, pool_w, pool_scale, mkv, w_out, h, l, mix)
        else:
            h = _swa_layer(a, sinks[l - n_pool], cos_t, sin_a, sin_b, kdup, vlo, vhi, mkv, w_out, h, l, mix)
        last = l == depth - 1
        h = _ffn(h, norms[l, 2][None, :], ffn_gate, ffn_up, ffn_down, l, 1,
                 final_gain=final_norm[None, :] if last else None)
        if l == n_pool - 1:
            kv = _norm_proj(h, kv_norm[None, :], w_kv[None], 0)
            kdup, vlo, vhi = _kv_layout(kv, cos_t, sin_a, sin_b)
    return h[None]
```

```python
import functools

import jax
import jax.numpy as jnp
from jax import lax
from jax.experimental import pallas as pl
from jax.experimental.pallas import tpu as pltpu

F32 = jnp.float32
BF16 = jnp.bfloat16

POOL_WINDOWS = (2, 4, 8, 16)
HEAD_DIM = 64
GQA_GROUP = 8
ATTN_BLOCK = 128
MEM_HEADS = 4
ROPE_THETA = 10000.0
EPS = 1e-5

LANES = 128
MAX_POOL_WINDOW = max(POOL_WINDOWS)

ROW_TILE = 512
FFN_ROW_TILE = 1024
FF_TILE = 512
VMEM_LIMIT = 48 * 1024 * 1024
FFN_VMEM_LIMIT = 56 * 1024 * 1024
LAYER_VMEM_LIMIT = 58 * 1024 * 1024


def _params(*sem):
    return pltpu.CompilerParams(dimension_semantics=sem, vmem_limit_bytes=VMEM_LIMIT)


def _layer_params():
    return pltpu.CompilerParams(dimension_semantics=("arbitrary",), vmem_limit_bytes=LAYER_VMEM_LIMIT)


def _rms(x, gain):
    y = x * lax.rsqrt(jnp.mean(x * x, axis=-1, keepdims=True) + EPS)
    return y * gain


def _ffn_body(h_hbm, g_ref, wg_ref, wu_ref, wd_ref, fg_ref, o_hbm, acc_ref, xn_ref, in_sem, out_sem,
              *, n_rows, n_ff, final_norm):
    i = pl.program_id(0)
    j = pl.program_id(1)
    tm = acc_ref.shape[1]
    slot = lax.rem(i, 2)
    other = 1 - slot

    def in_copy(tile, s):
        return pltpu.make_async_copy(h_hbm.at[pl.ds(tile * tm, tm), :], acc_ref.at[s], in_sem.at[s])

    def out_copy(tile, s):
        return pltpu.make_async_copy(acc_ref.at[s], o_hbm.at[pl.ds(tile * tm, tm), :], out_sem.at[s])

    @pl.when((i == 0) & (j == 0))
    def _():
        in_copy(0, 0).start()

    @pl.when(j == 0)
    def _():
        in_copy(i, slot).wait()
        xn_ref[...] = _rms(acc_ref[slot], g_ref[...]).astype(BF16)

    @pl.when((j == 1) & (i + 1 < n_rows))
    def _():
        @pl.when(i >= 1)
        def _():
            out_copy(i - 1, other).wait()

        in_copy(i + 1, other).start()

    xn = xn_ref[...]
    g = jnp.dot(xn, wg_ref[...].astype(BF16), preferred_element_type=F32)
    u = jnp.dot(xn, wu_ref[...].astype(BF16), preferred_element_type=F32)
    act = (g / (1.0 + jnp.exp(-g))) * (0.5 * u)
    acc_ref[slot] += jnp.dot(act.astype(BF16), wd_ref[...].astype(BF16), preferred_element_type=F32)

    @pl.when(j == n_ff - 1)
    def _():
        if final_norm:
            acc_ref[slot] = _rms(acc_ref[slot], fg_ref[...])
        out_copy(i, slot).start()

        @pl.when(i == n_rows - 1)
        def _():
            out_copy(i, slot).wait()
            if n_rows >= 2:
                out_copy(i - 1, other).wait()


def _ffn(h, gain, wg, wu, wd, layer, half, final_gain=None):
    s, d = h.shape
    ff = wg.shape[-1]
    n_ff = ff // FF_TILE
    n_rows = s // FFN_ROW_TILE
    assert n_ff >= 2 and s % FFN_ROW_TILE == 0 and ff % FF_TILE == 0
    final_norm = final_gain is not None
    fg = final_gain if final_norm else gain
    body = functools.partial(_ffn_body, n_rows=n_rows, n_ff=n_ff, final_norm=final_norm)
    return pl.pallas_call(
        body,
        out_shape=jax.ShapeDtypeStruct((s, d), F32),
        grid=(n_rows, n_ff),
        in_specs=[
            pl.BlockSpec(memory_space=pl.ANY),
            pl.BlockSpec((1, d), lambda i, j: (0, 0)),
            pl.BlockSpec((None, None, d, FF_TILE), lambda i, j: (layer, half, 0, j)),
            pl.BlockSpec((None, None, d, FF_TILE), lambda i, j: (layer, half, 0, j)),
            pl.BlockSpec((None, None, FF_TILE, d), lambda i, j: (layer, half, j, 0)),
            pl.BlockSpec((1, d), lambda i, j: (0, 0)),
        ],
        out_specs=pl.BlockSpec(memory_space=pl.ANY),
        scratch_shapes=[
            pltpu.VMEM((2, FFN_ROW_TILE, d), F32),
            pltpu.VMEM((FFN_ROW_TILE, d), BF16),
            pltpu.SemaphoreType.DMA((2,)),
            pltpu.SemaphoreType.DMA((2,)),
        ],
        compiler_params=pltpu.CompilerParams(
            dimension_semantics=("arbitrary", "arbitrary"), vmem_limit_bytes=FFN_VMEM_LIMIT),
        name="ffn",
    )(h, gain, wg, wu, wd, fg)


def _norm_proj_body(h_ref, g_ref, w_ref, o_ref):
    xn = _rms(h_ref[...], g_ref[...]).astype(BF16)
    o_ref[...] = jnp.dot(xn, w_ref[...].astype(BF16), preferred_element_type=F32)


def _norm_proj(h, gain, w, layer):
    s, d = h.shape
    n = w.shape[-1]
    return pl.pallas_call(
        _norm_proj_body,
        out_shape=jax.ShapeDtypeStruct((s, n), F32),
        grid=(s // ROW_TILE,),
        in_specs=[
            pl.BlockSpec((ROW_TILE, d), lambda i: (i, 0)),
            pl.BlockSpec((1, d), lambda i: (0, 0)),
            pl.BlockSpec((None, d, n), lambda i: (layer, 0, 0), pipeline_mode=pl.Buffered(1)),
        ],
        out_specs=pl.BlockSpec((ROW_TILE, n), lambda i: (i, 0)),
        compiler_params=_params("parallel"),
        name="norm_proj",
    )(h, gain, w)


def _mem_kv_body(mem_ref, g_ref, w_ref, o_ref):
    mn = _rms(mem_ref[...], g_ref[...]).astype(BF16)
    o_ref[...] = jnp.dot(mn, w_ref[...].astype(BF16), preferred_element_type=F32).astype(BF16)


def _mem_kv(mem, gain, w):
    depth, d, n = w.shape
    m = mem.shape[0]
    return pl.pallas_call(
        _mem_kv_body,
        out_shape=jax.ShapeDtypeStruct((depth, m, n), BF16),
        grid=(depth,),
        in_specs=[
            pl.BlockSpec((m, d), lambda l: (0, 0)),
            pl.BlockSpec((1, d), lambda l: (0, 0)),
            pl.BlockSpec((None, d, n), lambda l: (l, 0, 0)),
        ],
        out_specs=pl.BlockSpec((None, m, n), lambda l: (l, 0, 0)),
        compiler_params=_params("parallel"),
        name="mem_kv",
    )(mem, gain, w)


def _mem_scores(a_ref, mkv_ref, mix, mem_width):
    hd = mem_width // MEM_HEADS
    scale = hd ** -0.5
    scores = []
    for head in range(MEM_HEADS):
        lo = mix + hd * head
        q = a_ref[:, lo:lo + hd].astype(BF16)
        k = mkv_ref[:, hd * head:hd * (head + 1)]
        scores.append(lax.dot_general(q, k, (((1,), (1,)), ((), ())), preferred_element_type=F32) * scale)
    return scores


def _mem_finish(scores, mkv_ref, y_ref, mix, mem_width):
    hd = mem_width // MEM_HEADS
    for head in range(MEM_HEADS):
        lo = mix + hd * head
        s = scores[head]
        v = mkv_ref[:, mem_width + hd * head:mem_width + hd * (head + 1)]
        p = jnp.exp(s - jnp.max(s, axis=-1, keepdims=True))
        p = p * (1.0 / jnp.sum(p, axis=-1, keepdims=True))
        o = jnp.dot(p.astype(BF16), v, preferred_element_type=F32)
        y_ref[:, lo:lo + hd] = o.astype(y_ref.dtype)


def _out_proj_prev(y_ref, wo_ref, h_ref, o_ref):
    o_ref[...] = h_ref[...] + jnp.dot(y_ref[...], wo_ref[...].astype(BF16), preferred_element_type=F32)


def _pool_mix_body(a_ref, pw_ref, ps_ref, mkv_ref, wo_ref, h_ref, o_ref, carry_ref, y_ref, *, mix, mem_width):
    i = pl.program_id(0)
    tm = a_ref.shape[0]
    cg = mix // len(POOL_WINDOWS)

    @pl.when(i == 0)
    def _():
        carry_ref[...] = jnp.zeros_like(carry_ref)
        y_ref[...] = jnp.zeros_like(y_ref)

    mem_scores = _mem_scores(a_ref, mkv_ref, mix, mem_width)
    _out_proj_prev(y_ref, wo_ref, h_ref, o_ref)

    t1 = (i * tm + lax.broadcasted_iota(jnp.int32, (tm, 1), 0) + 1).astype(F32)
    for grp, w in enumerate(POOL_WINDOWS):
        sl = slice(grp * cg, (grp + 1) * cg)
        ug = a_ref[:, sl]
        win = jnp.concatenate([carry_ref[:, sl], ug], axis=0)
        shift = 1
        while shift < w:
            win = win + pltpu.roll(win, shift, 0)
            shift *= 2
        pooled = win[MAX_POOL_WINDOW:, :] * (1.0 / jnp.minimum(t1, float(w)))
        diff = (pooled - ug).astype(BF16)
        yg = jnp.dot(diff, pw_ref[grp].astype(BF16), preferred_element_type=F32) * ps_ref[:, sl]
        y_ref[:, sl] = yg.astype(y_ref.dtype)
        carry_ref[:, sl] = ug[tm - MAX_POOL_WINDOW:, :]

    _mem_finish(mem_scores, mkv_ref, y_ref, mix, mem_width)


def _pool_layer(a, pool_w, pool_scale, mkv, w_out, h, layer, mix):
    s, d = a.shape
    mem_width = d - mix
    m = mkv.shape[1]
    ng, cg, _ = pool_w.shape[1:]
    n_tiles = s // ROW_TILE
    body = functools.partial(_pool_mix_body, mix=mix, mem_width=mem_width)
    return pl.pallas_call(
        body,
        out_shape=jax.ShapeDtypeStruct((s, d), F32),
        grid=(n_tiles + 1,),
        in_specs=[
            pl.BlockSpec((ROW_TILE, d), lambda i: (jnp.minimum(i, n_tiles - 1), 0)),
            pl.BlockSpec((None, ng, cg, cg), lambda i: (layer, 0, 0, 0)),
            pl.BlockSpec((None, 1, mix), lambda i: (layer, 0, 0)),
            pl.BlockSpec((None, m, 2 * mem_width), lambda i: (layer, 0, 0)),
            pl.BlockSpec((None, d, d), lambda i: (layer, 0, 0), pipeline_mode=pl.Buffered(1)),
            pl.BlockSpec((ROW_TILE, d), lambda i: (jnp.maximum(i - 1, 0), 0)),
        ],
        out_specs=pl.BlockSpec((ROW_TILE, d), lambda i: (jnp.maximum(i - 1, 0), 0)),
        scratch_shapes=[pltpu.VMEM((MAX_POOL_WINDOW, mix), F32), pltpu.VMEM((ROW_TILE, d), BF16)],
        compiler_params=_layer_params(),
        name="pool_layer",
    )(a, pool_w, pool_scale[:, None, :], mkv, w_out, h)


def _trig_body(pos_ref, invf_ref, c_ref, sa_ref, sb_ref):
    ang = pos_ref[...] * invf_ref[...]
    c = jnp.cos(ang)
    s = jnp.sin(ang)
    lane = lax.broadcasted_iota(jnp.int32, ang.shape, 1)
    first_half = (lane & (HEAD_DIM // 2)) == 0
    c_ref[...] = c
    sa_ref[...] = jnp.where(first_half, -s, 0.0)
    sb_ref[...] = jnp.where(first_half, 0.0, s)


def _trig_tables(pos, invf):
    s = pos.shape[0]
    tab = jax.ShapeDtypeStruct((s, LANES), F32)
    row = pl.BlockSpec((ROW_TILE, LANES), lambda i: (i, 0))
    return pl.pallas_call(
        _trig_body,
        out_shape=(tab, tab, tab),
        grid=(s // ROW_TILE,),
        in_specs=[
            pl.BlockSpec((ROW_TILE, 1), lambda i: (i, 0)),
            pl.BlockSpec((1, LANES), lambda i: (0, 0)),
        ],
        out_specs=(row, row, row),
        compiler_params=_params("parallel"),
        name="trig_tables",
    )(pos, invf)


def _rope_slab(x, c, sa, sb):
    half = HEAD_DIM // 2
    return x * c + pltpu.roll(x, LANES - half, 1) * sa + pltpu.roll(x, half, 1) * sb


def _kv_layout_body(kv_ref, c_ref, sa_ref, sb_ref, kdup_ref, vlo_ref, vhi_ref):
    c, sa, sb = c_ref[...], sa_ref[...], sb_ref[...]
    lane = lax.broadcasted_iota(jnp.int32, c.shape, 1)
    left = lane < HEAD_DIM
    slab0 = kv_ref[:, 0:LANES]
    slab1 = kv_ref[:, LANES:2 * LANES]
    slab2 = kv_ref[:, 2 * LANES:3 * LANES]
    r0 = _rope_slab(slab0, c, sa, sb)
    r1 = _rope_slab(slab1, c, sa, sb)
    r0s = pltpu.roll(r0, HEAD_DIM, 1)
    r1s = pltpu.roll(r1, HEAD_DIM, 1)
    s1s = pltpu.roll(slab1, HEAD_DIM, 1)
    s2s = pltpu.roll(slab2, HEAD_DIM, 1)
    zero = jnp.zeros_like(c)
    kdup_ref[0] = jnp.where(left, r0, r0s).astype(BF16)
    kdup_ref[1] = jnp.where(left, r0s, r0).astype(BF16)
    kdup_ref[2] = jnp.where(left, r1, r1s).astype(BF16)
    vlo_ref[0] = jnp.where(left, s1s, zero).astype(BF16)
    vhi_ref[0] = jnp.where(left, zero, slab1).astype(BF16)
    vlo_ref[1] = jnp.where(left, slab2, zero).astype(BF16)
    vhi_ref[1] = jnp.where(left, zero, s2s).astype(BF16)
    vlo_ref[2] = jnp.where(left, s2s, zero).astype(BF16)
    vhi_ref[2] = jnp.where(left, zero, slab2).astype(BF16)


def _kv_layout(kv, c, sa, sb):
    s, n = kv.shape
    n_kv = n // (2 * HEAD_DIM)
    assert n == 3 * LANES and n_kv == 3
    out = jax.ShapeDtypeStruct((n_kv, s, LANES), BF16)
    row = pl.BlockSpec((ROW_TILE, LANES), lambda i: (i, 0))
    slab = pl.BlockSpec((n_kv, ROW_TILE, LANES), lambda i: (0, i, 0))
    return pl.pallas_call(
        _kv_layout_body,
        out_shape=(out, out, out),
        grid=(s // ROW_TILE,),
        in_specs=[pl.BlockSpec((ROW_TILE, n), lambda i: (i, 0)), row, row, row],
        out_specs=(slab, slab, slab),
        compiler_params=_params("parallel"),
        name="kv_layout",
    )(kv, c, sa, sb)


def _swa_mix_body(sink_ref, a_ref, c_ref, sa_ref, sb_ref,
                  kp_ref, kc_ref, vlp_ref, vlc_ref, vhp_ref, vhc_ref, mkv_ref, wo_ref, h_ref,
                  o_ref, y_ref, qlo_ref, qhi_ref, kf_ref, vlf_ref, vhf_ref, *, mix, mem_width):
    i = pl.program_id(0)

    @pl.when(i == 0)
    def _():
        y_ref[...] = jnp.zeros_like(y_ref)

    mem_scores = _mem_scores(a_ref, mkv_ref, mix, mem_width)
    _out_proj_prev(y_ref, wo_ref, h_ref, o_ref)
    tm = a_ref.shape[0]
    blk = ATTN_BLOCK
    nb = tm // blk
    n_kv = kc_ref.shape[0]
    pairs_per_kv = GQA_GROUP // 2
    n_pairs = mix // LANES

    kf_ref[:, 0:blk, :] = kp_ref[...]
    kf_ref[:, blk:, :] = kc_ref[...]
    vlf_ref[:, 0:blk, :] = vlp_ref[...]
    vlf_ref[:, blk:, :] = vlc_ref[...]
    vhf_ref[:, 0:blk, :] = vhp_ref[...]
    vhf_ref[:, blk:, :] = vhc_ref[...]

    c, sa, sb = c_ref[...], sa_ref[...], sb_ref[...]
    lane = lax.broadcasted_iota(jnp.int32, c.shape, 1)
    left = lane < HEAD_DIM
    scale = HEAD_DIM ** -0.5
    for p in range(n_pairs):
        sl = slice(p * LANES, (p + 1) * LANES)
        q = _rope_slab(a_ref[:, sl], c, sa, sb) * scale
        qlo_ref[:, sl] = jnp.where(left, q, 0.0).astype(BF16)
        qhi_ref[:, sl] = jnp.where(left, 0.0, q).astype(BF16)

    own = (lax.broadcasted_iota(jnp.int32, (blk, blk), 1)
           <= lax.broadcasted_iota(jnp.int32, (blk, blk), 0))

    def block_scores(n):
        rows = slice(n * blk, (n + 1) * blk)
        keys = slice(n * blk, (n + 2) * blk)
        scores = []
        for kvh in range(n_kv):
            p0 = kvh * pairs_per_kv
            q_all = jnp.concatenate(
                [qlo_ref[rows, (p0 + p) * LANES:(p0 + p + 1) * LANES] for p in range(pairs_per_kv)]
                + [qhi_ref[rows, (p0 + p) * LANES:(p0 + p + 1) * LANES] for p in range(pairs_per_kv)],
                axis=0)
            scores.append(lax.dot_general(q_all, kf_ref[kvh, keys, :], (((1,), (1,)), ((), ())),
                                          preferred_element_type=F32))
        return scores

    def block_finish(n, scores):
        rows = slice(n * blk, (n + 1) * blk)
        keys = slice(n * blk, (n + 2) * blk)
        for kvh in range(n_kv):
            p0 = kvh * pairs_per_kv
            s_all = scores[kvh]
            probs = []
            for idx in range(GQA_GROUP):
                half, p = divmod(idx, pairs_per_kv)
                sink = sink_ref[2 * (p0 + p) + half]
                s_head = s_all[idx * blk:(idx + 1) * blk, :]
                before = s_head[:, :blk]
                if n == 0:
                    before = before + jnp.where(i == 0, -jnp.inf, 0.0)
                s = jnp.where(own, s_head[:, blk:], before)
                m = jnp.maximum(jnp.max(s, axis=-1, keepdims=True), sink)
                e = jnp.exp(s - m)
                denom = jnp.sum(e, axis=-1, keepdims=True) + jnp.exp(sink - m)
                pr = e * (1.0 / denom)
                probs.append(jnp.concatenate(
                    [jnp.where(own, 0.0, pr), jnp.where(own, pr, 0.0)], axis=-1).astype(BF16))
            p_lo = jnp.concatenate(probs[:pairs_per_kv], axis=0)
            p_hi = jnp.concatenate(probs[pairs_per_kv:], axis=0)
            o = (jnp.dot(p_lo, vlf_ref[kvh, keys, :], preferred_element_type=F32)
                 + jnp.dot(p_hi, vhf_ref[kvh, keys, :], preferred_element_type=F32))
            for p in range(pairs_per_kv):
                y_ref[rows, (p0 + p) * LANES:(p0 + p + 1) * LANES] = (
                    o[p * blk:(p + 1) * blk, :].astype(y_ref.dtype))

    _mem_finish(mem_scores, mkv_ref, y_ref, mix, mem_width)
    pending = block_scores(0)
    for n in range(nb):
        upcoming = block_scores(n + 1) if n + 1 < nb else None
        block_finish(n, pending)
        pending = upcoming


def _swa_layer(a, sinks, c, sa, sb, kdup, vlo, vhi, mkv, w_out, h, layer, mix):
    s, d = a.shape
    mem_width = d - mix
    m = mkv.shape[1]
    n_kv = kdup.shape[0]
    nb = ROW_TILE // ATTN_BLOCK
    n_tiles = s // ROW_TILE
    body = functools.partial(_swa_mix_body, mix=mix, mem_width=mem_width)

    def tile(i):
        return jnp.minimum(i, n_tiles - 1)

    row = pl.BlockSpec((ROW_TILE, LANES), lambda i: (tile(i), 0))
    prev = pl.BlockSpec((n_kv, ATTN_BLOCK, LANES), lambda i: (0, jnp.maximum(tile(i) * nb - 1, 0), 0))
    cur = pl.BlockSpec((n_kv, ROW_TILE, LANES), lambda i: (0, tile(i), 0))
    full = pltpu.VMEM((n_kv, ROW_TILE + ATTN_BLOCK, LANES), BF16)
    return pl.pallas_call(
        body,
        out_shape=jax.ShapeDtypeStruct((s, d), F32),
        grid=(n_tiles + 1,),
        in_specs=[
            pl.BlockSpec(memory_space=pltpu.SMEM),
            pl.BlockSpec((ROW_TILE, d), lambda i: (tile(i), 0)),
            row, row, row,
            prev, cur, prev, cur, prev, cur,
            pl.BlockSpec((None, m, 2 * mem_width), lambda i: (layer, 0, 0)),
            pl.BlockSpec((None, d, d), lambda i: (layer, 0, 0), pipeline_mode=pl.Buffered(1)),
            pl.BlockSpec((ROW_TILE, d), lambda i: (jnp.maximum(i - 1, 0), 0)),
        ],
        out_specs=pl.BlockSpec((ROW_TILE, d), lambda i: (jnp.maximum(i - 1, 0), 0)),
        scratch_shapes=[
            pltpu.VMEM((ROW_TILE, d), BF16),
            pltpu.VMEM((ROW_TILE, mix), BF16),
            pltpu.VMEM((ROW_TILE, mix), BF16),
            full, full, full,
        ],
        compiler_params=_layer_params(),
        name="swa_layer",
    )(sinks, a, c, sa, sb, kdup, kdup, vlo, vlo, vhi, vhi, mkv, w_out, h)


def kernel(x, mem, positions, norms, ffn_gate, ffn_up, ffn_down, w_in, w_out, pool_w, pool_scale,
           w_kv, kv_norm, sinks, w_mem_kv, mem_norm, final_norm):
    batch, seq, d = x.shape
    assert batch == 1
    depth = norms.shape[0]
    n_pool = pool_w.shape[0]
    mix = pool_scale.shape[-1]

    half = HEAD_DIM // 2
    inv_freq = 1.0 / (ROPE_THETA ** (jnp.arange(half, dtype=F32) * (2.0 / HEAD_DIM)))
    invf = jnp.tile(inv_freq, LANES // half)[None, :]
    pos = positions[0].astype(F32)[:, None]
    cos_t, sin_a, sin_b = _trig_tables(pos, invf)

    mkv = _mem_kv(mem[0], mem_norm[None, :], w_mem_kv)

    h = x[0]
    kdup = vlo = vhi = None
    for l in range(depth):
        h = _ffn(h, norms[l, 0][None, :], ffn_gate, ffn_up, ffn_down, l, 0)
        a = _norm_proj(h, norms[l, 1][None, :], w_in, l)
        if l < n_pool:
            h = _pool_layer(a, pool_w, pool_scale, mkv, w_out, h, l, mix)
        else:
            h = _swa_layer(a, sinks[l - n_pool], cos_t, sin_a, sin_b, kdup, vlo, vhi, mkv, w_out, h, l, mix)
        last = l == depth - 1
        h = _ffn(h, norms[l, 2][None, :], ffn_gate, ffn_up, ffn_down, l, 1,
                 final_gain=final_norm[None, :] if last else None)
        if l == n_pool - 1:
            kv = _norm_proj(h, kv_norm[None, :], w_kv[None], 0)
            kdup, vlo, vhi = _kv_layout(kv, cos_t, sin_a, sin_b)
    return h[None]
```
